```python
import math
import jax, jax.numpy as jnp
from jax import lax
import numpy as np

D_MODEL = 1024
BATCH = 4
SEQ = 8192
DEPTH = 4

SSD_HEAD_DIM = 64
D_SSD = D_MODEL
SSD_HEADS = D_SSD // SSD_HEAD_DIM
SSD_GROUPS = 2
D_STATE = 128
CONV_W = 4
SSD_CHUNK = 128
CONV_DIM = D_SSD + 2 * SSD_GROUPS * D_STATE
DSA_HEADS = 8
DSA_HEAD_DIM = 64
D_DSA = DSA_HEADS * DSA_HEAD_DIM
D_LAT = 128
IDX_HEADS = 4
IDX_DIM = 64
TOPK_MAX = 256
Q_BLOCK = 128
MEM_LEN = 256
MEM_HEADS = 4
MEM_HEAD_DIM = 128
D_MEMG = MEM_HEADS * MEM_HEAD_DIM
D_MIX = D_SSD + D_DSA + D_MEMG
SPLIT_SIZES = (D_SSD,
               CONV_DIM,
               SSD_HEADS,
               DSA_HEADS * D_LAT,
               D_LAT,
               IDX_HEADS * IDX_DIM,
               IDX_DIM,
               IDX_HEADS,
               D_MEMG)
N_IN = sum(SPLIT_SIZES)
N_EXPERTS = 32
TOP_K = 4
D_FF = D_MODEL
SWIGLU_LIMIT = 7.0
SWIGLU_ALPHA = 1.702
EXPERT_BLOCK = 256
ALPHA = (2 * DEPTH) ** 0.25
BETA = (8 * DEPTH) ** -0.25
LN_EPS = 1e-5
RMS_EPS = 1e-6

kernel_name = "hymba_ssd_dsa_mem_moe_deepnorm"

F32 = jnp.float32


def layer_norm(x, g, b):
    xf = x.astype(F32)
    mu = jnp.mean(xf, -1, keepdims=True)
    xc = xf - mu
    var = jnp.mean(xc * xc, -1, keepdims=True)
    return (xc * lax.rsqrt(var + LN_EPS) * g + b).astype(x.dtype)


def rms_norm(x, g):
    xf = x.astype(F32)
    return (xf * lax.rsqrt(jnp.mean(xf * xf, -1, keepdims=True) + RMS_EPS) * g).astype(x.dtype)


def gated_rms_norm(y, z, g):
    b, s, dd = y.shape
    hf = (y * jax.nn.silu(z)).astype(F32).reshape(b, s, SSD_GROUPS, dd // SSD_GROUPS)
    hf = hf * lax.rsqrt(jnp.mean(hf * hf, -1, keepdims=True) + RMS_EPS)
    return (hf.reshape(b, s, dd) * g).astype(y.dtype)


def split_cols(p):
    outs, o = [], 0
    for n in SPLIT_SIZES:
        outs.append(p[..., o:o + n])
        o += n
    return outs


def alibi_slopes(n):
    return jnp.exp2(-8.0 * jnp.arange(1, n + 1, dtype=F32) / n)


def causal_depthwise_conv(u, w, bias):
    c = u.shape[-1]
    y = lax.conv_general_dilated(u, w[:, None, :].astype(u.dtype), window_strides=(1,),
                                 padding=[(CONV_W - 1, 0)],
                                 dimension_numbers=('NWC', 'WIO', 'NWC'),
                                 feature_group_count=c)
    return y + bias


def ssd_chunked_scan(xdt, adt, bmat, cmat):
    b, s, h, p = xdt.shape
    g, n = bmat.shape[-2:]
    k = h // g
    q = SSD_CHUNK
    nc = s // q
    to_chunks = lambda t: jnp.moveaxis(t.reshape((b, nc, q) + t.shape[2:]), 1, 0)
    xc = to_chunks(xdt.reshape(b, s, g, k, p))
    ac = to_chunks(adt.reshape(b, s, g, k))
    bc = to_chunks(bmat)
    cc = to_chunks(cmat)
    causal = jnp.tril(jnp.ones((q, q), bool))[None, :, :, None, None]

    def step(state, inp):
        x_, a_, b_, c_ = inp
        acum = jnp.cumsum(a_, axis=1)
        seg = acum[:, :, None] - acum[:, None]
        lmat = jnp.exp(jnp.where(causal, seg, -jnp.inf))
        cb = jnp.einsum('blgn,bsgn->blsg', c_, b_)
        y_diag = jnp.einsum('blsg,blsgk,bsgkp->blgkp', cb, lmat, x_)
        y_off = jnp.einsum('blgn,bgkpn->blgkp', c_, state) * jnp.exp(acum)[..., None]
        decay_end = jnp.exp(acum[:, -1:] - acum)
        new_state = state * jnp.exp(acum[:, -1])[..., None, None] + \
            jnp.einsum('bsgn,bsgk,bsgkp->bgkpn', b_, decay_end, x_)
        return new_state.astype(state.dtype), (y_diag + y_off).astype(x_.dtype)

    state0 = jnp.zeros((b, g, k, p, n), xdt.dtype)
    _, ys = lax.scan(step, state0, (xc, ac, bc, cc))
    return jnp.moveaxis(ys, 0, 1).reshape(b, s, h, p)


def ssd_group(z, xbc, dt_raw, conv_w, conv_b, dt_bias, a_log, d_skip, norm_g):
    b, s, _ = z.shape
    xbc = jax.nn.silu(causal_depthwise_conv(xbc, conv_w, conv_b))
    xs = xbc[..., :D_SSD].reshape(b, s, SSD_HEADS, SSD_HEAD_DIM)
    bm = xbc[..., D_SSD:D_SSD + SSD_GROUPS * D_STATE].reshape(b, s, SSD_GROUPS, D_STATE)
    cm = xbc[..., D_SSD + SSD_GROUPS * D_STATE:].reshape(b, s, SSD_GROUPS, D_STATE)
    dt = jax.nn.softplus(dt_raw + dt_bias)
    a = -jnp.exp(a_log)
    y = ssd_chunked_scan(xs * dt[..., None], dt * a, bm, cm)
    y = (y + d_skip[:, None] * xs).reshape(b, s, D_SSD)
    return gated_rms_norm(y, z, norm_g)


def dsa_group(q_lat, c_kv, q_idx, k_idx, w_idx, kv_norm_g, w_uv):
    b, s, _ = q_lat.shape
    q_lat = q_lat.reshape(b, s, DSA_HEADS, D_LAT)
    q_idx = q_idx.reshape(b, s, IDX_HEADS, IDX_DIM)
    c = rms_norm(c_kv, kv_norm_g)
    n_sel = min(TOPK_MAX, s // 4)
    nb = s // Q_BLOCK
    slopes = alibi_slopes(DSA_HEADS)
    key_pos = jnp.arange(s)
    scale = D_LAT ** -0.5

    def block(args):
        qb, qib, wb, t0 = args
        tq = t0 + jnp.arange(Q_BLOCK)
        idx_logits = jnp.einsum('bthd,bsd->bths', qib, k_idx)
        score = jnp.einsum('bth,bths->bts', wb, jax.nn.relu(idx_logits)).astype(F32)
        score = jnp.where(key_pos[None, None, :] <= tq[None, :, None], score, -jnp.inf)
        _, sel = lax.top_k(score, n_sel)
        c_sel = jax.vmap(lambda cc, ii: cc[ii])(c, sel)
        logits = jnp.einsum('bthc,btkc->bthk', qb, c_sel).astype(F32) * scale
        dist = (tq[None, :, None] - sel).astype(F32)
        logits = logits - slopes[None, None, :, None] * dist[:, :, None, :]
        logits = jnp.where((dist >= 0)[:, :, None, :], logits, -jnp.inf)
        p = jax.nn.softmax(logits, -1).astype(c_sel.dtype)
        return jnp.einsum('bthk,btkc->bthc', p, c_sel)

    blocks = lambda t: jnp.moveaxis(t.reshape((b, nb, Q_BLOCK) + t.shape[2:]), 1, 0)
    ctx = lax.map(block, (blocks(q_lat), blocks(q_idx), blocks(w_idx), jnp.arange(nb) * Q_BLOCK))
    ctx = jnp.moveaxis(ctx, 0, 1).reshape(b, s, DSA_HEADS, D_LAT)
    return jnp.einsum('bshc,hcd->bshd', ctx, w_uv).reshape(b, s, D_DSA)


def memory_group(q_mem, mem, w_mem_k, w_mem_v):
    b, s, _ = q_mem.shape
    m = mem.shape[1]
    q = q_mem.reshape(b, s, MEM_HEADS, MEM_HEAD_DIM)
    k = (mem @ w_mem_k).reshape(b, m, MEM_HEADS, MEM_HEAD_DIM)
    v = (mem @ w_mem_v).reshape(b, m, MEM_HEADS, MEM_HEAD_DIM)
    logits = jnp.einsum('bthd,bmhd->bhtm', q, k).astype(F32) * (MEM_HEAD_DIM ** -0.5)
    p = jax.nn.softmax(logits, -1).astype(v.dtype)
    return jnp.einsum('bhtm,bmhd->bthd', p, v).reshape(b, s, D_MEMG)


def hybrid_mixer(h, mem, w_in, conv_w, conv_b, dt_bias, a_log, d_skip, ssd_norm_g,
                 kv_norm_g, w_uv, w_mem_k, w_mem_v, w_out):
    proj = h @ w_in
    z, xbc, dt_raw, q_lat, c_kv, q_idx, k_idx, w_idx, q_mem = split_cols(proj)
    y_ssd = ssd_group(z, xbc, dt_raw, conv_w, conv_b, dt_bias, a_log, d_skip, ssd_norm_g)
    y_dsa = dsa_group(q_lat, c_kv, q_idx, k_idx, w_idx, kv_norm_g, w_uv)
    y_mem = memory_group(q_mem, mem, w_mem_k, w_mem_v)
    return jnp.concatenate([y_ssd, y_dsa, y_mem], -1) @ w_out


def clamped_swiglu(gu):
    glu, lin = jnp.split(gu, 2, axis=-1)
    glu = jnp.minimum(glu, SWIGLU_LIMIT)
    lin = jnp.clip(lin, -SWIGLU_LIMIT, SWIGLU_LIMIT)
    return glu * jax.nn.sigmoid(SWIGLU_ALPHA * glu) * (lin + 1.0)


def moe(h2, router_w, router_b, w_gu, b_gu, w_down, b_down):
    t, d = h2.shape
    logits = (h2 @ router_w + router_b).astype(F32)
    top_val, top_idx = lax.top_k(logits, TOP_K)
    gates = jax.nn.softmax(top_val, -1).astype(h2.dtype)
    flat_e = top_idx.reshape(-1)
    tk = flat_e.shape[0]
    order = jnp.argsort(flat_e)
    sorted_e = flat_e[order]
    sorted_tok = (order // TOP_K).astype(jnp.int32)
    counts = jnp.bincount(flat_e, length=N_EXPERTS)
    padded = (counts + EXPERT_BLOCK - 1) // EXPERT_BLOCK * EXPERT_BLOCK
    start_sorted = jnp.cumsum(counts) - counts
    pad_end = jnp.cumsum(padded)
    start_pad = pad_end - padded
    dest = (start_pad[sorted_e] + jnp.arange(tk) - start_sorted[sorted_e]).astype(jnp.int32)
    n_blk = -(-tk // EXPERT_BLOCK) + N_EXPERTS
    row_tok = jnp.zeros((n_blk * EXPERT_BLOCK,), jnp.int32).at[dest].set(sorted_tok)
    blk_e = jnp.minimum(jnp.searchsorted(pad_end, jnp.arange(n_blk) * EXPERT_BLOCK, side='right'),
                        N_EXPERTS - 1)

    def expert_block(args):
        toks, e = args
        xb = h2[toks]
        gu = xb @ w_gu[e] + b_gu[e]
        return clamped_swiglu(gu) @ w_down[e] + b_down[e]

    ys = lax.map(expert_block, (row_tok.reshape(n_blk, EXPERT_BLOCK), blk_e)).reshape(-1, d)
    dest_orig = jnp.zeros((tk,), jnp.int32).at[order].set(dest)
    y_sel = ys[dest_orig].reshape(t, TOP_K, d)
    return jnp.einsum('tk,tkd->td', gates, y_sel)


def setup_inputs(seed: int = 0) -> dict:
    key = jax.random.key(seed)
    ks = jax.random.split(key, 32)
    L = DEPTH
    nrm = lambda k, shape, sc: jax.random.normal(k, shape, F32) * sc
    dt = jnp.exp(jax.random.uniform(ks[5], (L, SSD_HEADS), F32, math.log(1e-3), math.log(1e-1)))
    return {
        "x": nrm(ks[0], (BATCH, SEQ, D_MODEL), 1.0),
        "mem": nrm(ks[1], (BATCH, MEM_LEN, D_MODEL), 1.0),
        "ln_in_g": 1.0 + nrm(ks[2], (D_MODEL,), 0.02),
        "ln_in_b": nrm(ks[3], (D_MODEL,), 0.02),
        "w_in": nrm(ks[4], (L, D_MODEL, N_IN), D_MODEL ** -0.5),
        "conv_w": nrm(ks[6], (L, CONV_W, CONV_DIM), CONV_W ** -0.5),
        "conv_b": nrm(ks[7], (L, CONV_DIM), 0.02),
        "dt_bias": dt + jnp.log(-jnp.expm1(-dt)),
        "a_log": jnp.log(jax.random.uniform(ks[8], (L, SSD_HEADS), F32, 1.0, 16.0)),
        "d_skip": 1.0 + nrm(ks[9], (L, SSD_HEADS), 0.02),
        "ssd_norm_g": 1.0 + nrm(ks[10], (L, D_SSD), 0.02),
        "kv_norm_g": 1.0 + nrm(ks[11], (L, D_LAT), 0.02),
        "w_uv": nrm(ks[12], (L, DSA_HEADS, D_LAT, DSA_HEAD_DIM), BETA * D_LAT ** -0.5),
        "w_mem_k": nrm(ks[13], (L, D_MODEL, D_MEMG), D_MODEL ** -0.5),
        "w_mem_v": nrm(ks[14], (L, D_MODEL, D_MEMG), BETA * D_MODEL ** -0.5),
        "w_out": nrm(ks[15], (L, D_MIX, D_MODEL), BETA * D_MIX ** -0.5),
        "ln1_g": 1.0 + nrm(ks[16], (L, D_MODEL), 0.02),
        "ln1_b": nrm(ks[17], (L, D_MODEL), 0.02),
        "router_w": nrm(ks[18], (L, D_MODEL, N_EXPERTS), D_MODEL ** -0.5),
        "router_b": nrm(ks[19], (L, N_EXPERTS), 0.01),
        "w_gu": nrm(ks[20], (L, N_EXPERTS, D_MODEL, 2 * D_FF), BETA * D_MODEL ** -0.5),
        "b_gu": nrm(ks[21], (L, N_EXPERTS, 2 * D_FF), 0.02),
        "w_down": nrm(ks[22], (L, N_EXPERTS, D_FF, D_MODEL), BETA * D_FF ** -0.5),
        "b_down": nrm(ks[23], (L, N_EXPERTS, D_MODEL), 0.02),
        "ln2_g": 1.0 + nrm(ks[24], (L, D_MODEL), 0.02),
        "ln2_b": nrm(ks[25], (L, D_MODEL), 0.02),
    }


def reference(x, mem, ln_in_g, ln_in_b, w_in, conv_w, conv_b, dt_bias, a_log, d_skip,
              ssd_norm_g, kv_norm_g, w_uv, w_mem_k, w_mem_v, w_out, ln1_g, ln1_b,
              router_w, router_b, w_gu, b_gu, w_down, b_down, ln2_g, ln2_b):
    b, s, d = x.shape
    h = layer_norm(x, ln_in_g, ln_in_b)
    for l in range(DEPTH):
        mix = hybrid_mixer(h, mem, w_in[l], conv_w[l], conv_b[l], dt_bias[l], a_log[l],
                           d_skip[l], ssd_norm_g[l], kv_norm_g[l], w_uv[l], w_mem_k[l],
                           w_mem_v[l], w_out[l])
        h = layer_norm(ALPHA * h + mix, ln1_g[l], ln1_b[l])
        ff = moe(h.reshape(b * s, d), router_w[l], router_b[l], w_gu[l], b_gu[l],
                 w_down[l], b_down[l]).reshape(b, s, d)
        h = layer_norm(ALPHA * h + ff, ln2_g[l], ln2_b[l])
    return h
```

```python
import functools

import jax
import jax.numpy as jnp
from jax import lax
from jax.experimental import pallas as pl
from jax.experimental.pallas import tpu as pltpu

F32, BF16, I32 = jnp.float32, jnp.bfloat16, jnp.int32
HIGHEST = lax.Precision.HIGHEST

D_MODEL = 1024
SSD_HEADS, SSD_HEAD_DIM, SSD_GROUPS, D_STATE, CONV_W, SSD_CHUNK = 16, 64, 2, 128, 4, 128
D_SSD = SSD_HEADS * SSD_HEAD_DIM
CONV_DIM = D_SSD + 2 * SSD_GROUPS * D_STATE
DSA_HEADS, DSA_HEAD_DIM, D_LAT, IDX_HEADS, IDX_DIM, TOPK_MAX = 8, 64, 128, 4, 64, 256
D_DSA = DSA_HEADS * DSA_HEAD_DIM
MEM_HEADS, MEM_HEAD_DIM = 4, 128
D_MEMG = MEM_HEADS * MEM_HEAD_DIM
D_MIX = D_SSD + D_DSA + D_MEMG
N_EXPERTS, TOP_K, D_FF = 32, 4, 1024
SWIGLU_LIMIT, SWIGLU_ALPHA = 7.0, 1.702
EXPERT_BLOCK = 256
LN_EPS, RMS_EPS = 1e-5, 1e-6

LANES = 128
SUBLANES = 8
VMEM_LIMIT = 56 * 1024 * 1024
NEG_BIG = -1e30
KEY_NEG_INF = -2139095041

_O_Z, _O_XBC, _O_DT = 0, D_SSD, D_SSD + CONV_DIM
_O_QLAT = _O_DT + SSD_HEADS
_O_CKV = _O_QLAT + DSA_HEADS * D_LAT
_O_QIDX = _O_CKV + D_LAT
_O_KIDX = _O_QIDX + IDX_HEADS * IDX_DIM
_O_WIDX = _O_KIDX + IDX_DIM
_O_QMEM = _O_WIDX + IDX_HEADS
N_IN = _O_QMEM + D_MEMG
W_ZX = D_SSD + CONV_DIM
W_Q = DSA_HEADS * D_LAT + IDX_HEADS * LANES + D_MEMG + LANES
W_MISC = 2 * LANES
DTW_WIDX = SSD_HEADS


def _params(n_axes):
    return pltpu.CompilerParams(dimension_semantics=("arbitrary",) * n_axes, vmem_limit_bytes=VMEM_LIMIT)


def _layer_norm(x, g, b):
    mu = jnp.mean(x, -1, keepdims=True)
    xc = x - mu
    var = jnp.mean(xc * xc, -1, keepdims=True)
    return xc * lax.rsqrt(var + LN_EPS) * g + b


def _dot(a, b):
    return jnp.dot(a, b, preferred_element_type=F32)


def _dot_nt(a, b):
    return lax.dot_general(a, b, (((1,), (1,)), ((), ())), preferred_element_type=F32)


def _ln_kernel(x_ref, g_ref, b_ref, o_ref):
    o_ref[...] = _layer_norm(x_ref[...], g_ref[...], b_ref[...])


def _ln_call(x, g, b, tm=512):
    t, d = x.shape
    return pl.pallas_call(
        _ln_kernel, out_shape=jax.ShapeDtypeStruct((t, d), F32), grid=(t // tm,),
        in_specs=[pl.BlockSpec((tm, d), lambda i: (i, 0)), pl.BlockSpec((1, d), lambda i: (0, 0)),
                  pl.BlockSpec((1, d), lambda i: (0, 0))],
        out_specs=pl.BlockSpec((tm, d), lambda i: (i, 0)), compiler_params=_params(1), name="ln_in",
    )(x, g.reshape(1, d), b.reshape(1, d))


def _inproj_kernel(h_ref, wzx_ref, wq_ref, wm_ref, kvg_ref,
                   zx_ref, qlat_ref, qidx_ref, qmem_ref, kidx_ref, c_ref, dtw_ref):
    hb = h_ref[...].astype(BF16)
    zx_ref[...] = _dot(hb, wzx_ref[...])
    q = _dot(hb, wq_ref[...])
    o1 = DSA_HEADS * D_LAT
    o2 = o1 + IDX_HEADS * LANES
    o3 = o2 + D_MEMG
    qlat_ref[...] = q[:, :o1].astype(BF16)
    qidx_ref[...] = q[:, o1:o2].astype(BF16)
    qmem_ref[...] = q[:, o2:o3].astype(BF16)
    kidx_ref[...] = q[:, o3:].astype(BF16)
    misc = _dot(hb, wm_ref[...])
    ckv = misc[:, :D_LAT]
    c = ckv * lax.rsqrt(jnp.mean(ckv * ckv, -1, keepdims=True) + RMS_EPS) * kvg_ref[...]
    c_ref[...] = c.astype(BF16)
    dtw_ref[...] = misc[:, D_LAT:]


def _inproj_call(h, wzx, wq, wm, kvg, tm=256):
    t = h.shape[0]
    row = lambda w: pl.BlockSpec((tm, w), lambda i: (i, 0))
    full = lambda a: pl.BlockSpec(a.shape, lambda i: (0,) * a.ndim)
    outs = [(W_ZX, F32), (DSA_HEADS * D_LAT, BF16), (IDX_HEADS * LANES, BF16), (D_MEMG, BF16),
            (LANES, BF16), (D_LAT, BF16), (LANES, F32)]
    return pl.pallas_call(
        _inproj_kernel,
        out_shape=[jax.ShapeDtypeStruct((t, w), dt) for w, dt in outs],
        grid=(t // tm,),
        in_specs=[row(D_MODEL), full(wzx), full(wq), full(wm), full(kvg)],
        out_specs=[row(w) for w, _ in outs],
        compiler_params=_params(1), name="in_proj",
    )(h, wzx, wq, wm, kvg)


def _ssd_kernel(zx_ref, dtw_ref, cw_ref, cb_ref, dtb_ref, alog_ref, dskip_ref, ng_ref, e64_ref,
                y_ref, st_ref, ext_ref):
    q = SSD_CHUNK
    halo = SUBLANES

    @pl.when(pl.program_id(1) == 0)
    def _():
        st_ref[...] = jnp.zeros_like(st_ref)
        ext_ref[0:halo, :] = jnp.zeros((halo, CONV_DIM), F32)

    ext_ref[halo:halo + q, :] = zx_ref[:, D_SSD:]
    conv = cb_ref[...] + cw_ref[0:1, :] * ext_ref[halo - 3:halo - 3 + q, :]
    for j in range(1, CONV_W):
        conv = conv + cw_ref[j:j + 1, :] * ext_ref[halo - 3 + j:halo - 3 + j + q, :]
    ext_ref[0:halo, :] = ext_ref[q:q + halo, :]
    xbc = conv * jax.nn.sigmoid(conv)
    xs = xbc[:, :D_SSD]

    lane = lax.broadcasted_iota(I32, (q, LANES), 1)
    head_lane = lane < SSD_HEADS
    dtr = dtw_ref[...] + dtb_ref[...]
    dt = jnp.where(head_lane, jnp.maximum(dtr, 0.0) + jnp.log(1.0 + jnp.exp(-jnp.abs(dtr))), 0.0)
    adt = jnp.where(head_lane, dt * -jnp.exp(alog_ref[...]), 0.0)
    ri = lax.broadcasted_iota(I32, (q, q), 0)
    ci = lax.broadcasted_iota(I32, (q, q), 1)
    causal = ri >= ci
    acum = jnp.dot(causal.astype(F32), adt, precision=HIGHEST, preferred_element_type=F32)
    full = jnp.dot(jnp.concatenate([dt, acum], 0), e64_ref[...], precision=HIGHEST, preferred_element_type=F32)
    dt_full, acum_full = full[:q], full[q:]
    alast = acum_full[q - 1:q, :]
    eacum = jnp.exp(acum_full)
    dend = jnp.exp(alast - acum_full)
    dlast = jnp.exp(alast)
    xdt = xs * dt_full
    acum_t = acum.T
    lo_half = lane < SSD_HEAD_DIM

    ys = []
    hpg = SSD_HEADS // SSD_GROUPS
    for g in range(SSD_GROUPS):
        bg = xbc[:, D_SSD + g * D_STATE:D_SSD + (g + 1) * D_STATE]
        cg = xbc[:, D_SSD + (SSD_GROUPS + g) * D_STATE:D_SSD + (SSD_GROUPS + g + 1) * D_STATE].astype(BF16)
        bgt = bg.T.astype(BF16)
        cbg = _dot(cg, bgt)
        for pp in range(hpg // 2):
            pair = g * (hpg // 2) + pp
            sl = slice(pair * LANES, (pair + 1) * LANES)
            ws = []
            for hh in (2 * pair, 2 * pair + 1):
                seg = acum[:, hh:hh + 1] - acum_t[hh:hh + 1, :]
                lmat = jnp.exp(jnp.where(causal, seg, -jnp.inf))
                ws.append((cbg * lmat).astype(BF16))
            xp = xdt[:, sl]
            x_lo = jnp.where(lo_half, xp, 0.0).astype(BF16)
            x_hi = jnp.where(lo_half, 0.0, xp).astype(BF16)
            y_diag = _dot(jnp.concatenate(ws, 1), jnp.concatenate([x_lo, x_hi], 0))
            st = st_ref[pair]
            y_off = _dot(cg, st.astype(BF16)) * eacum[:, sl]
            ys.append(y_diag + y_off + dskip_ref[:, sl] * xs[:, sl])
            st_ref[pair] = st * dlast[:, sl] + _dot(bgt, (xp * dend[:, sl]).astype(BF16))
    y = jnp.concatenate(ys, 1)

    z = zx_ref[:, :D_SSD]
    hf = y * (z * jax.nn.sigmoid(z))
    gw = D_SSD // SSD_GROUPS
    outs = []
    for g in range(SSD_GROUPS):
        part = hf[:, g * gw:(g + 1) * gw]
        outs.append(part * lax.rsqrt(jnp.mean(part * part, -1, keepdims=True) + RMS_EPS))
    y_ref[...] = (jnp.concatenate(outs, 1) * ng_ref[...]).astype(BF16)


def _ssd_call(zx, dtw, cw, cb, dtb, alog, dskip, ng, e64, batch):
    t = zx.shape[0]
    q = SSD_CHUNK
    nc = t // batch // q
    blk = lambda w: pl.BlockSpec((q, w), lambda b, c: (b * nc + c, 0))
    full = lambda a: pl.BlockSpec(a.shape, lambda b, c: (0,) * a.ndim)
    return pl.pallas_call(
        _ssd_kernel, out_shape=jax.ShapeDtypeStruct((t, D_SSD), BF16), grid=(batch, nc),
        in_specs=[blk(W_ZX), blk(LANES)] + [full(a) for a in (cw, cb, dtb, alog, dskip, ng, e64)],
        out_specs=blk(D_SSD),
        scratch_shapes=[pltpu.VMEM((SSD_HEADS // 2, D_STATE, LANES), F32),
                        pltpu.VMEM((q + 2 * SUBLANES, CONV_DIM), F32)],
        compiler_params=_params(2), name="ssd",
    )(zx, dtw, cw, cb, dtb, alog, dskip, ng, e64)


def _dsa_kernel(qlat_ref, qidx_ref, dtw_ref, c_ref, kidx_ref, wuv_ref, o_ref,
                key_ref, thr_ref, need_ref, m_ref, l_ref, acc_ref, *, tq, rg, n_sel):
    i = pl.program_id(1)
    nkb = i + 1
    t0 = i * tq
    tk = tq
    n_lt = tk // LANES
    w = dtw_ref[...]

    def score_blk(j, _):
        kb = kidx_ref[pl.ds(pl.multiple_of(j * tk, tk), tk), :]
        sc = jnp.zeros((tq, tk), F32)
        for h in range(IDX_HEADS):
            lg = _dot_nt(qidx_ref[:, h * LANES:(h + 1) * LANES], kb)
            sc = sc + w[:, DTW_WIDX + h:DTW_WIDX + h + 1] * jnp.maximum(lg, 0.0)
        rows = t0 + lax.broadcasted_iota(I32, (tq, tk), 0)
        cols = j * tk + lax.broadcasted_iota(I32, (tq, tk), 1)
        sc = jnp.where(cols <= rows, sc, -jnp.inf)
        bits = pltpu.bitcast(sc, I32)
        key = bits ^ ((bits >> 31) & jnp.int32(0x7FFFFFFF))
        key_ref[j] = jnp.where(key == -1, 0, key)
        return 0

    lax.fori_loop(0, nkb, score_blk, 0)

    def count_ge(g0, cand):
        def blk(j, acc):
            t = key_ref[j, pl.ds(g0, rg), :]
            for u in range(n_lt):
                acc = acc + (t[:, u * LANES:(u + 1) * LANES] >= cand).astype(I32)
            return acc
        acc = lax.fori_loop(0, nkb, blk, jnp.zeros((rg, LANES), I32))
        cnt = jnp.sum(acc.astype(F32), axis=1, keepdims=True)
        return jnp.broadcast_to(cnt, (rg, LANES))

    def search_group(g, _):
        g0 = pl.multiple_of(g * rg, rg)

        def bit_body(b, r):
            cand = r + jnp.left_shift(jnp.int32(1), 31 - b)
            return jnp.where(count_ge(g0, cand) >= n_sel, cand, r)

        r = lax.fori_loop(0, 32, bit_body, jnp.full((rg, LANES), jnp.iinfo(jnp.int32).min, I32))
        thr_ref[pl.ds(g0, rg), :] = r
        need_ref[pl.ds(g0, rg), :] = n_sel - count_ge(g0, r + 1)
        return 0

    lax.fori_loop(0, tq // rg, search_group, 0)

    ri = lax.broadcasted_iota(I32, (tk, tk), 0)
    ci = lax.broadcasted_iota(I32, (tk, tk), 1)
    upper = jnp.where(ri <= ci, 1.0, 0.0).astype(BF16)

    def mask_group(g, _):
        g0 = pl.multiple_of(g * rg, rg)
        r = jnp.concatenate([thr_ref[pl.ds(g0, rg), :]] * n_lt, 1)
        need = jnp.concatenate([need_ref[pl.ds(g0, rg), :]] * n_lt, 1)

        def blk(j, carry):
            t = key_ref[j, pl.ds(g0, rg), :]
            eq = t == r
            pre = _dot(jnp.where(eq, 1.0, 0.0).astype(BF16), upper) + carry
            tie = jnp.where(eq, jnp.where(pre <= need, 0.0, NEG_BIG), NEG_BIG)
            bias = jnp.where(t > r, 0.0, tie)
            bias = jnp.where(t == KEY_NEG_INF, NEG_BIG, bias)
            key_ref[j, pl.ds(g0, rg), :] = pltpu.bitcast(bias, I32)
            return jnp.broadcast_to(pre[:, tk - 1:tk], (rg, tk))

        lax.fori_loop(0, nkb, blk, jnp.zeros((rg, tk), F32))
        return 0

    lax.fori_loop(0, tq // rg, mask_group, 0)

    m_ref[...] = jnp.full(m_ref.shape, NEG_BIG, F32)
    l_ref[...] = jnp.zeros(l_ref.shape, F32)
    acc_ref[...] = jnp.zeros(acc_ref.shape, F32)
    scale = D_LAT ** -0.5

    def attn_blk(j, _):
        cb = c_ref[pl.ds(pl.multiple_of(j * tk, tk), tk), :]
        bias = pltpu.bitcast(key_ref[j], F32)
        colpos = (j * tk - t0 + lax.broadcasted_iota(I32, (1, tk), 1)).astype(F32)
        for h in range(DSA_HEADS):
            slope = 2.0 ** (-8.0 * (h + 1) / DSA_HEADS)
            s = _dot_nt(qlat_ref[:, h * D_LAT:(h + 1) * D_LAT], cb)
            x = s * scale + slope * colpos + bias
            m_prev = m_ref[h]
            m_next = jnp.maximum(m_prev, jnp.max(x, axis=1, keepdims=True))
            alpha = jnp.exp(m_prev - m_next)
            p = jnp.exp(x - jnp.concatenate([m_next] * n_lt, 1))
            l_ref[h] = alpha * l_ref[h] + jnp.sum(p, axis=1, keepdims=True)
            acc_ref[h] = alpha * acc_ref[h] + _dot(p.astype(BF16), cb)
            m_ref[h] = m_next
        return 0

    lax.fori_loop(0, nkb, attn_blk, 0)

    out = jnp.zeros((tq, D_DSA), F32)
    for h in range(DSA_HEADS):
        ctx = acc_ref[h] / l_ref[h]
        out = out + _dot(ctx.astype(BF16), wuv_ref[h])
    o_ref[...] = out.astype(BF16)


def _dsa_call(qlat, qidx, dtw, c, kidx, wuv, batch, tq=256, rg=64):
    t = qlat.shape[0]
    s = t // batch
    nq = s // tq
    n_sel = min(TOPK_MAX, s // 4)
    blk = lambda w: pl.BlockSpec((tq, w), lambda b, i: (b * nq + i, 0))
    per_b = lambda w: pl.BlockSpec((s, w), lambda b, i: (b, 0))
    return pl.pallas_call(
        functools.partial(_dsa_kernel, tq=tq, rg=rg, n_sel=n_sel),
        out_shape=jax.ShapeDtypeStruct((t, D_DSA), BF16), grid=(batch, nq),
        in_specs=[blk(DSA_HEADS * D_LAT), blk(IDX_HEADS * LANES), blk(LANES), per_b(D_LAT), per_b(LANES),
                  pl.BlockSpec(wuv.shape, lambda b, i: (0, 0, 0))],
        out_specs=blk(D_DSA),
        scratch_shapes=[pltpu.VMEM((nq, tq, tq), I32), pltpu.VMEM((tq, LANES), I32), pltpu.VMEM((tq, LANES), F32),
                        pltpu.VMEM((DSA_HEADS, tq, LANES), F32), pltpu.VMEM((DSA_HEADS, tq, LANES), F32),
                        pltpu.VMEM((DSA_HEADS, tq, D_LAT), F32)],
        compiler_params=_params(2), name="dsa",
    )(qlat, qidx, dtw, c, kidx, wuv)


def _memkv_kernel(mem_ref, wk_ref, wv_ref, k_ref, v_ref):
    mb = mem_ref[...].astype(BF16)
    k_ref[...] = _dot(mb, wk_ref[...]).astype(BF16)
    v_ref[...] = _dot(mb, wv_ref[...]).astype(BF16)


def _memkv_call(mem2, wk, wv, batch):
    m = mem2.shape[0] // batch
    blk = lambda w: pl.BlockSpec((m, w), lambda b: (b, 0))
    full = lambda a: pl.BlockSpec(a.shape, lambda b: (0, 0))
    return pl.pallas_call(
        _memkv_kernel, out_shape=[jax.ShapeDtypeStruct((mem2.shape[0], D_MEMG), BF16)] * 2, grid=(batch,),
        in_specs=[blk(D_MODEL), full(wk), full(wv)], out_specs=[blk(D_MEMG)] * 2,
        compiler_params=_params(1), name="mem_kv",
    )(mem2, wk, wv)


def _mix_kernel(yssd_ref, ydsa_ref, qmem_ref, km_ref, vm_ref, h_ref, wo_ref, g_ref, b_ref, rwt_ref, rb_ref,
                h1_ref, eidx_ref, gate_ref, rank_ref, cnt_ref, carry_ref, *, alpha, tm):
    @pl.when(pl.program_id(0) == 0)
    def _():
        carry_ref[...] = jnp.zeros_like(carry_ref)

    scale = MEM_HEAD_DIM ** -0.5
    ymem = []
    for hd in range(MEM_HEADS):
        sl = slice(hd * MEM_HEAD_DIM, (hd + 1) * MEM_HEAD_DIM)
        lg = _dot_nt(qmem_ref[:, sl], km_ref[:, sl]) * scale
        p = jnp.exp(lg - jnp.max(lg, axis=1, keepdims=True))
        pv = _dot(p.astype(BF16), vm_ref[:, sl])
        ymem.append((pv / jnp.sum(p, axis=1, keepdims=True)).astype(BF16))
    mix = _dot(yssd_ref[...], wo_ref[0:D_SSD, :]) + _dot(ydsa_ref[...], wo_ref[D_SSD:D_SSD + D_DSA, :])
    mix = mix + _dot(jnp.concatenate(ymem, 1), wo_ref[D_SSD + D_DSA:, :])
    h1 = _layer_norm(alpha * h_ref[...] + mix, g_ref[...], b_ref[...])
    h1_ref[...] = h1

    lt = _dot_nt(rwt_ref[...], h1.astype(BF16)) + rb_ref[:, 0:1]
    eio = lax.broadcasted_iota(I32, (N_EXPERTS, tm), 0)
    vals, idxs, hots = [], [], []
    cur = lt
    for _ in range(TOP_K):
        mx = jnp.max(cur, axis=0, keepdims=True)
        ix = jnp.min(jnp.where(cur == mx, eio, N_EXPERTS), axis=0, keepdims=True)
        hot = eio == ix
        vals.append(mx)
        idxs.append(ix)
        hots.append(hot)
        cur = jnp.where(hot, -jnp.inf, cur)
    ex = [jnp.exp(v - vals[0]) for v in vals]
    den = ex[0] + ex[1] + ex[2] + ex[3]
    multi = jnp.zeros((N_EXPERTS, tm), F32)
    for hot in hots:
        multi = multi + jnp.where(hot, 1.0, 0.0)
    ri = lax.broadcasted_iota(I32, (tm, tm), 0)
    ci = lax.broadcasted_iota(I32, (tm, tm), 1)
    before = jnp.where(ri < ci, 1.0, 0.0).astype(BF16)
    carry = carry_ref[...]
    pos = _dot(multi.astype(BF16), before) + carry[:, 0:1]
    for k in range(TOP_K):
        eidx_ref[k:k + 1, :] = idxs[k]
        gate_ref[k:k + 1, :] = ex[k] / den
        rank_ref[k:k + 1, :] = jnp.sum(jnp.where(hots[k], pos, 0.0), axis=0, keepdims=True).astype(I32)
    carry = carry + jnp.sum(multi, axis=1, keepdims=True)
    carry_ref[...] = carry
    cnt_ref[...] = carry.astype(I32)


def _mix_call(yssd, ydsa, qmem, km, vm, h, wo, g, b, rwt, rb, batch, alpha, tm=256):
    t = h.shape[0]
    spb = t // batch // tm
    m = km.shape[0] // batch
    row = lambda w: pl.BlockSpec((tm, w), lambda i: (i, 0))
    full = lambda a: pl.BlockSpec(a.shape, lambda i: (0,) * a.ndim)
    memb = pl.BlockSpec((m, D_MEMG), lambda i: (i // spb, 0))
    tok = pl.BlockSpec((TOP_K, tm), lambda i: (0, i))
    return pl.pallas_call(
        functools.partial(_mix_kernel, alpha=alpha, tm=tm),
        out_shape=[jax.ShapeDtypeStruct((t, D_MODEL), F32), jax.ShapeDtypeStruct((TOP_K, t), I32),
                   jax.ShapeDtypeStruct((TOP_K, t), F32), jax.ShapeDtypeStruct((TOP_K, t), I32),
                   jax.ShapeDtypeStruct((N_EXPERTS, LANES), I32)],
        grid=(t // tm,),
        in_specs=[row(D_SSD), row(D_DSA), row(D_MEMG), memb, memb, row(D_MODEL), full(wo), full(g), full(b),
                  full(rwt), full(rb)],
        out_specs=[row(D_MODEL), tok, tok, tok, pl.BlockSpec((N_EXPERTS, LANES), lambda i: (0, 0))],
        scratch_shapes=[pltpu.VMEM((N_EXPERTS, LANES), F32)],
        compiler_params=_params(1), name="mix_ln1_router",
    )(yssd, ydsa, qmem, km, vm, h, wo, g, b, rwt, rb)


def _dispatch_kernel(dest_ref, h_hbm, xs_in, xs_hbm, sem, *, tm):
    del xs_in
    base = pl.program_id(0) * tm

    def issue(tt, _):
        for k in range(TOP_K):
            pltpu.make_async_copy(h_hbm.at[pl.ds(base + tt, 1)], xs_hbm.at[pl.ds(dest_ref[k, tt], 1)], sem).start()
        return 0

    lax.fori_loop(0, tm, issue, 0)

    def drain(tt, _):
        for k in range(TOP_K):
            pltpu.make_async_copy(h_hbm.at[pl.ds(0, 1)], xs_hbm.at[pl.ds(0, 1)], sem).wait()
        return 0

    lax.fori_loop(0, tm, drain, 0)


def _dispatch_call(dest, h1, n_rows, tm=128):
    t, d = h1.shape
    xs0 = jnp.zeros((n_rows, d), F32)
    return pl.pallas_call(
        functools.partial(_dispatch_kernel, tm=tm),
        out_shape=jax.ShapeDtypeStruct((n_rows, d), F32), grid=(t // tm,),
        in_specs=[pl.BlockSpec((TOP_K, tm), lambda i: (0, i), memory_space=pltpu.SMEM),
                  pl.BlockSpec(memory_space=pl.ANY), pl.BlockSpec(memory_space=pl.ANY)],
        out_specs=pl.BlockSpec(memory_space=pl.ANY),
        scratch_shapes=[pltpu.SemaphoreType.DMA(())],
        input_output_aliases={2: 0},
        compiler_params=pltpu.CompilerParams(dimension_semantics=("arbitrary",), has_side_effects=True),
        name="moe_dispatch",
    )(dest, h1, xs0)


def _moe_kernel(be_ref, nu_ref, x_ref, wgu_ref, bgu_ref, wd_ref, bd_ref, y_ref):
    i = pl.program_id(0)

    @pl.when(i < nu_ref[0])
    def _():
        gu = _dot(x_ref[...].astype(BF16), wgu_ref[...]) + bgu_ref[...]
        glu = jnp.minimum(gu[:, :D_FF], SWIGLU_LIMIT)
        lin = jnp.clip(gu[:, D_FF:], -SWIGLU_LIMIT, SWIGLU_LIMIT)
        act = glu * jax.nn.sigmoid(SWIGLU_ALPHA * glu) * (lin + 1.0)
        y_ref[...] = _dot(act.astype(BF16), wd_ref[...]) + bd_ref[...]

    @pl.when(i >= nu_ref[0])
    def _():
        y_ref[...] = jnp.zeros_like(y_ref)


def _moe_call(blk_e, n_used, xs, wgu, bgu, wd, bd):
    n_rows, d = xs.shape
    n_blk = n_rows // EXPERT_BLOCK
    grid_spec = pltpu.PrefetchScalarGridSpec(
        num_scalar_prefetch=2, grid=(n_blk,),
        in_specs=[pl.BlockSpec((EXPERT_BLOCK, d), lambda i, be, nu: (i, 0)),
                  pl.BlockSpec((None, d, 2 * D_FF), lambda i, be, nu: (be[i], 0, 0)),
                  pl.BlockSpec((None, 1, 2 * D_FF), lambda i, be, nu: (be[i], 0, 0)),
                  pl.BlockSpec((None, D_FF, d), lambda i, be, nu: (be[i], 0, 0)),
                  pl.BlockSpec((None, 1, d), lambda i, be, nu: (be[i], 0, 0))],
        out_specs=pl.BlockSpec((EXPERT_BLOCK, d), lambda i, be, nu: (i, 0)))
    return pl.pallas_call(
        _moe_kernel, out_shape=jax.ShapeDtypeStruct((n_rows, d), F32), grid_spec=grid_spec,
        compiler_params=_params(1), name="moe_experts",
    )(blk_e, n_used, xs, wgu, bgu, wd, bd)


def _combine_kernel(dest_ref, ys_hbm, gate_ref, h1_ref, g_ref, b_ref, o_ref, buf_ref, sem, *, alpha, tm):
    def issue(tt, _):
        for k in range(TOP_K):
            pltpu.make_async_copy(ys_hbm.at[pl.ds(dest_ref[k, tt], 1)], buf_ref.at[k, pl.ds(tt, 1)], sem).start()
        return 0

    lax.fori_loop(0, tm, issue, 0)

    def drain(tt, _):
        for k in range(TOP_K):
            pltpu.make_async_copy(ys_hbm.at[pl.ds(0, 1)], buf_ref.at[0, pl.ds(0, 1)], sem).wait()
        return 0

    lax.fori_loop(0, tm, drain, 0)

    gt = gate_ref[...]
    ff = gt[:, 0:1] * buf_ref[0]
    for k in range(1, TOP_K):
        ff = ff + gt[:, k:k + 1] * buf_ref[k]
    o_ref[...] = _layer_norm(alpha * h1_ref[...] + ff, g_ref[...], b_ref[...])


def _combine_call(dest, ys, gates_t, h1, g, b, alpha, tm=128):
    t, d = h1.shape
    return pl.pallas_call(
        functools.partial(_combine_kernel, alpha=alpha, tm=tm),
        out_shape=jax.ShapeDtypeStruct((t, d), F32), grid=(t // tm,),
        in_specs=[pl.BlockSpec((TOP_K, tm), lambda i: (0, i), memory_space=pltpu.SMEM),
                  pl.BlockSpec(memory_space=pl.ANY),
                  pl.BlockSpec((tm, TOP_K), lambda i: (i, 0)), pl.BlockSpec((tm, d), lambda i: (i, 0)),
                  pl.BlockSpec((1, d), lambda i: (0, 0)), pl.BlockSpec((1, d), lambda i: (0, 0))],
        out_specs=pl.BlockSpec((tm, d), lambda i: (i, 0)),
        scratch_shapes=[pltpu.VMEM((TOP_K, tm, d), F32), pltpu.SemaphoreType.DMA(())],
        compiler_params=_params(1), name="moe_combine_ln2",
    )(dest, ys, gates_t, h1, g, b)


def _prep_in_weights(w_in):
    zero = lambda n: jnp.zeros(w_in.shape[:-1] + (n,), w_in.dtype)
    col = lambda o, n: w_in[..., o:o + n]
    qidx = []
    for hd in range(IDX_HEADS):
        qidx += [col(_O_QIDX + hd * IDX_DIM, IDX_DIM), zero(LANES - IDX_DIM)]
    wzx = col(_O_Z, W_ZX)
    wq = jnp.concatenate([col(_O_QLAT, DSA_HEADS * D_LAT)] + qidx +
                         [col(_O_QMEM, D_MEMG), col(_O_KIDX, IDX_DIM), zero(LANES - IDX_DIM)], -1)
    wm = jnp.concatenate([col(_O_CKV, D_LAT), col(_O_DT, SSD_HEADS), col(_O_WIDX, IDX_HEADS),
                          zero(LANES - SSD_HEADS - IDX_HEADS)], -1)
    return wzx.astype(BF16), wq.astype(BF16), wm.astype(BF16)


def _pad_lanes(a):
    return jnp.concatenate([a, jnp.zeros(a.shape[:-1] + (LANES - a.shape[-1],), a.dtype)], -1)


def kernel(x, mem, ln_in_g, ln_in_b, w_in, conv_w, conv_b, dt_bias, a_log, d_skip, ssd_norm_g, kv_norm_g, w_uv,
           w_mem_k, w_mem_v, w_out, ln1_g, ln1_b, router_w, router_b, w_gu, b_gu, w_down, b_down, ln2_g, ln2_b):
    batch, seq, d = x.shape
    depth = w_in.shape[0]
    t = batch * seq
    alpha = (2 * depth) ** 0.25
    n_slots = t * TOP_K
    n_blk = -(-n_slots // EXPERT_BLOCK) + N_EXPERTS
    n_rows = n_blk * EXPERT_BLOCK

    wzx, wq, wm = _prep_in_weights(w_in)
    e64 = (jnp.arange(LANES)[:, None] == jnp.arange(D_SSD)[None, :] // SSD_HEAD_DIM).astype(F32)
    dskip_full = jnp.repeat(d_skip, SSD_HEAD_DIM, axis=-1)
    eye = jnp.eye(DSA_HEADS, dtype=w_uv.dtype)
    wuv = (w_uv[:, :, :, None, :] * eye[None, :, None, :, None]).reshape(depth, DSA_HEADS, D_LAT, D_DSA).astype(BF16)
    mem2 = mem.reshape(batch * mem.shape[1], d)
    wmk, wmv, wo = w_mem_k.astype(BF16), w_mem_v.astype(BF16), w_out.astype(BF16)
    rwt = jnp.swapaxes(router_w, 1, 2).astype(BF16)
    rb = jnp.broadcast_to(router_b[:, :, None], (depth, N_EXPERTS, LANES))
    wgu, wd = w_gu.astype(BF16), w_down.astype(BF16)

    h = _ln_call(x.reshape(t, d), ln_in_g, ln_in_b)
    for l in range(depth):
        zx, qlat, qidx, qmem, kidx, c, dtw = _inproj_call(h, wzx[l], wq[l], wm[l], kv_norm_g[l].reshape(1, D_LAT))
        y_ssd = _ssd_call(zx, dtw, conv_w[l], conv_b[l].reshape(1, CONV_DIM), _pad_lanes(dt_bias[l].reshape(1, -1)),
                          _pad_lanes(a_log[l].reshape(1, -1)), dskip_full[l].reshape(1, D_SSD),
                          ssd_norm_g[l].reshape(1, D_SSD), e64, batch)
        y_dsa = _dsa_call(qlat, qidx, dtw, c, kidx, wuv[l], batch)
        km, vm = _memkv_call(mem2, wmk[l], wmv[l], batch)
        h1, eidx, gates, rank, counts = _mix_call(y_ssd, y_dsa, qmem, km, vm, h, wo[l], ln1_g[l].reshape(1, d),
                                                  ln1_b[l].reshape(1, d), rwt[l], rb[l], batch, alpha)
        cnt = counts[:, 0]
        padded = (cnt + EXPERT_BLOCK - 1) // EXPERT_BLOCK * EXPERT_BLOCK
        pad_end = jnp.cumsum(padded)
        dest = (jnp.take(pad_end - padded, eidx) + rank).astype(I32)
        blk_e = jnp.minimum(jnp.searchsorted(pad_end, jnp.arange(n_blk, dtype=I32) * EXPERT_BLOCK, side="right"),
                            N_EXPERTS - 1).astype(I32)
        n_used = (pad_end[-1:] // EXPERT_BLOCK).astype(I32)
        xs = _dispatch_call(dest, h1, n_rows)
        ys = _moe_call(blk_e, n_used, xs, wgu[l], b_gu[l].reshape(N_EXPERTS, 1, 2 * D_FF), wd[l],
                       b_down[l].reshape(N_EXPERTS, 1, d))
        h = _combine_call(dest, ys, gates.T, h1, ln2_g[l].reshape(1, d), ln2_b[l].reshape(1, d), alpha)
    return h.reshape(batch, seq, d)
```

```python
import functools

import jax
import jax.numpy as jnp
from jax import lax
from jax.experimental import pallas as pl
from jax.experimental.pallas import tpu as pltpu

F32, BF16, I32 = jnp.float32, jnp.bfloat16, jnp.int32
HIGHEST = lax.Precision.HIGHEST

D_MODEL = 1024
SSD_HEADS, SSD_HEAD_DIM, SSD_GROUPS, D_STATE, CONV_W, SSD_CHUNK = 16, 64, 2, 128, 4, 128
D_SSD = SSD_HEADS * SSD_HEAD_DIM
CONV_DIM = D_SSD + 2 * SSD_GROUPS * D_STATE
DSA_HEADS, DSA_HEAD_DIM, D_LAT, IDX_HEADS, IDX_DIM, TOPK_MAX = 8, 64, 128, 4, 64, 256
D_DSA = DSA_HEADS * DSA_HEAD_DIM
MEM_HEADS, MEM_HEAD_DIM = 4, 128
D_MEMG = MEM_HEADS * MEM_HEAD_DIM
D_MIX = D_SSD + D_DSA + D_MEMG
N_EXPERTS, TOP_K, D_FF = 32, 4, 1024
SWIGLU_LIMIT, SWIGLU_ALPHA = 7.0, 1.702
EXPERT_BLOCK = 256
LN_EPS, RMS_EPS = 1e-5, 1e-6

LANES = 128
SUBLANES = 8
VMEM_LIMIT = 56 * 1024 * 1024
NEG_BIG = -1e30
KEY_NEG_INF = -2139095041

_O_Z, _O_XBC, _O_DT = 0, D_SSD, D_SSD + CONV_DIM
_O_QLAT = _O_DT + SSD_HEADS
_O_CKV = _O_QLAT + DSA_HEADS * D_LAT
_O_QIDX = _O_CKV + D_LAT
_O_KIDX = _O_QIDX + IDX_HEADS * IDX_DIM
_O_WIDX = _O_KIDX + IDX_DIM
_O_QMEM = _O_WIDX + IDX_HEADS
N_IN = _O_QMEM + D_MEMG
W_ZX = D_SSD + CONV_DIM
W_Q = DSA_HEADS * D_LAT + IDX_HEADS * LANES + D_MEMG + LANES
W_MISC = 2 * LANES
DTW_WIDX = SSD_HEADS


def _params(n_axes):
    return pltpu.CompilerParams(dimension_semantics=("arbitrary",) * n_axes, vmem_limit_bytes=VMEM_LIMIT)


def _layer_norm(x, g, b):
    mu = jnp.mean(x, -1, keepdims=True)
    xc = x - mu
    var = jnp.mean(xc * xc, -1, keepdims=True)
    return xc * lax.rsqrt(var + LN_EPS) * g + b


def _dot(a, b):
    return jnp.dot(a, b, preferred_element_type=F32)


def _dot_nt(a, b):
    return lax.dot_general(a, b, (((1,), (1,)), ((), ())), preferred_element_type=F32)


def _ln_kernel(x_ref, g_ref, b_ref, o_ref):
    o_ref[...] = _layer_norm(x_ref[...], g_ref[...], b_ref[...])


def _ln_call(x, g, b, tm=512):
    t, d = x.shape
    return pl.pallas_call(
        _ln_kernel, out_shape=jax.ShapeDtypeStruct((t, d), F32), grid=(t // tm,),
        in_specs=[pl.BlockSpec((tm, d), lambda i: (i, 0)), pl.BlockSpec((1, d), lambda i: (0, 0)),
                  pl.BlockSpec((1, d), lambda i: (0, 0))],
        out_specs=pl.BlockSpec((tm, d), lambda i: (i, 0)), compiler_params=_params(1), name="ln_in",
    )(x, g.reshape(1, d), b.reshape(1, d))


def _inproj_kernel(h_ref, wzx_ref, wq_ref, wm_ref, kvg_ref,
                   zx_ref, qlat_ref, qidx_ref, qmem_ref, kidx_ref, c_ref, dtw_ref):
    hb = h_ref[...].astype(BF16)
    zx_ref[...] = _dot(hb, wzx_ref[...])
    q = _dot(hb, wq_ref[...])
    o1 = DSA_HEADS * D_LAT
    o2 = o1 + IDX_HEADS * LANES
    o3 = o2 + D_MEMG
    qlat_ref[...] = q[:, :o1].astype(BF16)
    qidx_ref[...] = q[:, o1:o2].astype(BF16)
    qmem_ref[...] = q[:, o2:o3].astype(BF16)
    kidx_ref[...] = q[:, o3:].astype(BF16)
    misc = _dot(hb, wm_ref[...])
    ckv = misc[:, :D_LAT]
    c = ckv * lax.rsqrt(jnp.mean(ckv * ckv, -1, keepdims=True) + RMS_EPS) * kvg_ref[...]
    c_ref[...] = c.astype(BF16)
    dtw_ref[...] = misc[:, D_LAT:]


def _inproj_call(h, wzx, wq, wm, kvg, tm=256):
    t = h.shape[0]
    row = lambda w: pl.BlockSpec((tm, w), lambda i: (i, 0))
    full = lambda a: pl.BlockSpec(a.shape, lambda i: (0,) * a.ndim)
    outs = [(W_ZX, F32), (DSA_HEADS * D_LAT, BF16), (IDX_HEADS * LANES, BF16), (D_MEMG, BF16),
            (LANES, BF16), (D_LAT, BF16), (LANES, F32)]
    return pl.pallas_call(
        _inproj_kernel,
        out_shape=[jax.ShapeDtypeStruct((t, w), dt) for w, dt in outs],
        grid=(t // tm,),
        in_specs=[row(D_MODEL), full(wzx), full(wq), full(wm), full(kvg)],
        out_specs=[row(w) for w, _ in outs],
        compiler_params=_params(1), name="in_proj",
    )(h, wzx, wq, wm, kvg)


def _ssd_kernel(zx_ref, dtw_ref, cw_ref, cb_ref, dtb_ref, alog_ref, dskip_ref, ng_ref, e64_ref,
                y_ref, st_ref, ext_ref):
    q = SSD_CHUNK
    halo = SUBLANES

    @pl.when(pl.program_id(1) == 0)
    def _():
        st_ref[...] = jnp.zeros_like(st_ref)
        ext_ref[0:halo, :] = jnp.zeros((halo, CONV_DIM), F32)

    ext_ref[halo:halo + q, :] = zx_ref[:, D_SSD:]
    conv = cb_ref[...] + cw_ref[0:1, :] * ext_ref[halo - 3:halo - 3 + q, :]
    for j in range(1, CONV_W):
        conv = conv + cw_ref[j:j + 1, :] * ext_ref[halo - 3 + j:halo - 3 + j + q, :]
    ext_ref[0:halo, :] = ext_ref[q:q + halo, :]
    xbc = conv * jax.nn.sigmoid(conv)
    xs = xbc[:, :D_SSD]

    lane = lax.broadcasted_iota(I32, (q, LANES), 1)
    head_lane = lane < SSD_HEADS
    dtr = dtw_ref[...] + dtb_ref[...]
    dt = jnp.where(head_lane, jnp.maximum(dtr, 0.0) + jnp.log(1.0 + jnp.exp(-jnp.abs(dtr))), 0.0)
    adt = jnp.where(head_lane, dt * -jnp.exp(alog_ref[...]), 0.0)
    ri = lax.broadcasted_iota(I32, (q, q), 0)
    ci = lax.broadcasted_iota(I32, (q, q), 1)
    causal = ri >= ci
    acum = jnp.dot(causal.astype(F32), adt, precision=HIGHEST, preferred_element_type=F32)
    full = jnp.dot(jnp.concatenate([dt, acum], 0), e64_ref[...], precision=HIGHEST, preferred_element_type=F32)
    dt_full, acum_full = full[:q], full[q:]
    alast = acum_full[q - 1:q, :]
    eacum = jnp.exp(acum_full)
    dend = jnp.exp(alast - acum_full)
    dlast = jnp.exp(alast)
    xdt = xs * dt_full
    acum_t = acum.T
    lo_half = lane < SSD_HEAD_DIM

    ys = []
    hpg = SSD_HEADS // SSD_GROUPS
    for g in range(SSD_GROUPS):
        bg = xbc[:, D_SSD + g * D_STATE:D_SSD + (g + 1) * D_STATE]
        cg = xbc[:, D_SSD + (SSD_GROUPS + g) * D_STATE:D_SSD + (SSD_GROUPS + g + 1) * D_STATE].astype(BF16)
        bgt = bg.T.astype(BF16)
        cbg = _dot(cg, bgt)
        for pp in range(hpg // 2):
            pair = g * (hpg // 2) + pp
            sl = slice(pair * LANES, (pair + 1) * LANES)
            ws = []
            for hh in (2 * pair, 2 * pair + 1):
                seg = acum[:, hh:hh + 1] - acum_t[hh:hh + 1, :]
                lmat = jnp.exp(jnp.where(causal, seg, -jnp.inf))
                ws.append((cbg * lmat).astype(BF16))
            xp = xdt[:, sl]
            x_lo = jnp.where(lo_half, xp, 0.0).astype(BF16)
            x_hi = jnp.where(lo_half, 0.0, xp).astype(BF16)
            y_diag = _dot(jnp.concatenate(ws, 1), jnp.concatenate([x_lo, x_hi], 0))
            st = st_ref[pair]
            y_off = _dot(cg, st.astype(BF16)) * eacum[:, sl]
            ys.append(y_diag + y_off + dskip_ref[:, sl] * xs[:, sl])
            st_ref[pair] = st * dlast[:, sl] + _dot(bgt, (xp * dend[:, sl]).astype(BF16))
    y = jnp.concatenate(ys, 1)

    z = zx_ref[:, :D_SSD]
    hf = y * (z * jax.nn.sigmoid(z))
    gw = D_SSD // SSD_GROUPS
    outs = []
    for g in range(SSD_GROUPS):
        part = hf[:, g * gw:(g + 1) * gw]
        outs.append(part * lax.rsqrt(jnp.mean(part * part, -1, keepdims=True) + RMS_EPS))
    y_ref[...] = (jnp.concatenate(outs, 1) * ng_ref[...]).astype(BF16)


def _ssd_call(zx, dtw, cw, cb, dtb, alog, dskip, ng, e64, batch):
    t = zx.shape[0]
    q = SSD_CHUNK
    nc = t // batch // q
    blk = lambda w: pl.BlockSpec((q, w), lambda b, c: (b * nc + c, 0))
    full = lambda a: pl.BlockSpec(a.shape, lambda b, c: (0,) * a.ndim)
    return pl.pallas_call(
        _ssd_kernel, out_shape=jax.ShapeDtypeStruct((t, D_SSD), BF16), grid=(batch, nc),
        in_specs=[blk(W_ZX), blk(LANES)] + [full(a) for a in (cw, cb, dtb, alog, dskip, ng, e64)],
        out_specs=blk(D_SSD),
        scratch_shapes=[pltpu.VMEM((SSD_HEADS // 2, D_STATE, LANES), F32),
                        pltpu.VMEM((q + 2 * SUBLANES, CONV_DIM), F32)],
        compiler_params=_params(2), name="ssd",
    )(zx, dtw, cw, cb, dtb, alog, dskip, ng, e64)


def _dsa_kernel(qlat_ref, qidx_ref, dtw_ref, c_ref, kidx_ref, wuv_ref, o_ref,
                key_ref, thr_ref, need_ref, m_ref, l_ref, acc_ref, *, tq, rg, n_sel):
    i = pl.program_id(1)
    nkb = i + 1
    t0 = i * tq
    tk = tq
    n_lt = tk // LANES
    w = dtw_ref[...]

    def score_blk(j, _):
        kb = kidx_ref[pl.ds(pl.multiple_of(j * tk, tk), tk), :]
        sc = jnp.zeros((tq, tk), F32)
        for h in range(IDX_HEADS):
            lg = _dot_nt(qidx_ref[:, h * LANES:(h + 1) * LANES], kb)
            sc = sc + w[:, DTW_WIDX + h:DTW_WIDX + h + 1] * jnp.maximum(lg, 0.0)
        rows = t0 + lax.broadcasted_iota(I32, (tq, tk), 0)
        cols = j * tk + lax.broadcasted_iota(I32, (tq, tk), 1)
        sc = jnp.where(cols <= rows, sc, -jnp.inf)
        bits = pltpu.bitcast(sc, I32)
        key = bits ^ ((bits >> 31) & jnp.int32(0x7FFFFFFF))
        key_ref[j] = jnp.where(key == -1, 0, key)
        return 0

    lax.fori_loop(0, nkb, score_blk, 0)

    def count_ge(g0, cand):
        def blk(j, acc):
            t = key_ref[j, pl.ds(g0, rg), :]
            for u in range(n_lt):
                acc = acc + (t[:, u * LANES:(u + 1) * LANES] >= cand).astype(I32)
            return acc
        acc = lax.fori_loop(0, nkb, blk, jnp.zeros((rg, LANES), I32))
        cnt = jnp.sum(acc.astype(F32), axis=1, keepdims=True)
        return jnp.broadcast_to(cnt, (rg, LANES))

    def search_group(g, _):
        g0 = pl.multiple_of(g * rg, rg)

        def bit_body(b, r):
            cand = r + jnp.left_shift(jnp.int32(1), 31 - b)
            return jnp.where(count_ge(g0, cand) >= n_sel, cand, r)

        r = lax.fori_loop(0, 32, bit_body, jnp.full((rg, LANES), jnp.iinfo(jnp.int32).min, I32))
        thr_ref[pl.ds(g0, rg), :] = r
        need_ref[pl.ds(g0, rg), :] = n_sel - count_ge(g0, r + 1)
        return 0

    lax.fori_loop(0, tq // rg, search_group, 0)

    ri = lax.broadcasted_iota(I32, (tk, tk), 0)
    ci = lax.broadcasted_iota(I32, (tk, tk), 1)
    upper = jnp.where(ri <= ci, 1.0, 0.0).astype(BF16)

    def mask_group(g, _):
        g0 = pl.multiple_of(g * rg, rg)
        r = jnp.concatenate([thr_ref[pl.ds(g0, rg), :]] * n_lt, 1)
        need = jnp.concatenate([need_ref[pl.ds(g0, rg), :]] * n_lt, 1)

        def blk(j, carry):
            t = key_ref[j, pl.ds(g0, rg), :]
            eq = t == r
            pre = _dot(jnp.where(eq, 1.0, 0.0).astype(BF16), upper) + carry
            tie = jnp.where(eq, jnp.where(pre <= need, 0.0, NEG_BIG), NEG_BIG)
            bias = jnp.where(t > r, 0.0, tie)
            bias = jnp.where(t == KEY_NEG_INF, NEG_BIG, bias)
            key_ref[j, pl.ds(g0, rg), :] = pltpu.bitcast(bias, I32)
            return jnp.broadcast_to(pre[:, tk - 1:tk], (rg, tk))

        lax.fori_loop(0, nkb, blk, jnp.zeros((rg, tk), F32))
        return 0

    lax.fori_loop(0, tq // rg, mask_group, 0)

    m_ref[...] = jnp.full(m_ref.shape, NEG_BIG, F32)
    l_ref[...] = jnp.zeros(l_ref.shape, F32)
    acc_ref[...] = jnp.zeros(acc_ref.shape, F32)
    scale = D_LAT ** -0.5

    def attn_blk(j, _):
        cb = c_ref[pl.ds(pl.multiple_of(j * tk, tk), tk), :]
        bias = pltpu.bitcast(key_ref[j], F32)
        colpos = (j * tk - t0 + lax.broadcasted_iota(I32, (1, tk), 1)).astype(F32)
        for h in range(DSA_HEADS):
            slope = 2.0 ** (-8.0 * (h + 1) / DSA_HEADS)
            s = _dot_nt(qlat_ref[:, h * D_LAT:(h + 1) * D_LAT], cb)
            x = s * scale + slope * colpos + bias
            m_prev = m_ref[h]
            m_next = jnp.maximum(m_prev, jnp.max(x, axis=1, keepdims=True))
            alpha = jnp.exp(m_prev - m_next)
            p = jnp.exp(x - jnp.concatenate([m_next] * n_lt, 1))
            l_ref[h] = alpha * l_ref[h] + jnp.sum(p, axis=1, keepdims=True)
            acc_ref[h] = alpha * acc_ref[h] + _dot(p.astype(BF16), cb)
            m_ref[h] = m_next
        return 0

    lax.fori_loop(0, nkb, attn_blk, 0)

    out = jnp.zeros((tq, D_DSA), F32)
    for h in range(DSA_HEADS):
        ctx = acc_ref[h] / l_ref[h]
        out = out + _dot(ctx.astype(BF16), wuv_ref[h])
    o_ref[...] = out.astype(BF16)


def _dsa_call(qlat, qidx, dtw, c, kidx, wuv, batch, tq=256, rg=64):
    t = qlat.shape[0]
    s = t // batch
    nq = s // tq
    n_sel = min(TOPK_MAX, s // 4)
    blk = lambda w: pl.BlockSpec((tq, w), lambda b, i: (b * nq + i, 0))
    per_b = lambda w: pl.BlockSpec((s, w), lambda b, i: (b, 0))
    return pl.pallas_call(
        functools.partial(_dsa_kernel, tq=tq, rg=rg, n_sel=n_sel),
        out_shape=jax.ShapeDtypeStruct((t, D_DSA), BF16), grid=(batch, nq),
        in_specs=[blk(DSA_HEADS * D_LAT), blk(IDX_HEADS * LANES), blk(LANES), per_b(D_LAT), per_b(LANES),
                  pl.BlockSpec(wuv.shape, lambda b, i: (0, 0, 0))],
        out_specs=blk(D_DSA),
        scratch_shapes=[pltpu.VMEM((nq, tq, tq), I32), pltpu.VMEM((tq, LANES), I32), pltpu.VMEM((tq, LANES), F32),
                        pltpu.VMEM((DSA_HEADS, tq, LANES), F32), pltpu.VMEM((DSA_HEADS, tq, LANES), F32),
                        pltpu.VMEM((DSA_HEADS, tq, D_LAT), F32)],
        compiler_params=_params(2), name="dsa",
    )(qlat, qidx, dtw, c, kidx, wuv)


def _memkv_kernel(mem_ref, wk_ref, wv_ref, k_ref, v_ref):
    mb = mem_ref[...].astype(BF16)
    k_ref[...] = _dot(mb, wk_ref[...]).astype(BF16)
    v_ref[...] = _dot(mb, wv_ref[...]).astype(BF16)


def _memkv_call(mem2, wk, wv, batch):
    m = mem2.shape[0] // batch
    blk = lambda w: pl.BlockSpec((m, w), lambda b: (b, 0))
    full = lambda a: pl.BlockSpec(a.shape, lambda b: (0, 0))
    return pl.pallas_call(
        _memkv_kernel, out_shape=[jax.ShapeDtypeStruct((mem2.shape[0], D_MEMG), BF16)] * 2, grid=(batch,),
        in_specs=[blk(D_MODEL), full(wk), full(wv)], out_specs=[blk(D_MEMG)] * 2,
        compiler_params=_params(1), name="mem_kv",
    )(mem2, wk, wv)


def _mix_kernel(yssd_ref, ydsa_ref, qmem_ref, km_ref, vm_ref, h_ref, wo_ref, g_ref, b_ref, rwt_ref, rb_ref,
                h1_ref, eidx_ref, gate_ref, rank_ref, cnt_ref, carry_ref, *, alpha, tm):
    @pl.when(pl.program_id(0) == 0)
    def _():
        carry_ref[...] = jnp.zeros_like(carry_ref)

    scale = MEM_HEAD_DIM ** -0.5
    ymem = []
    for hd in range(MEM_HEADS):
        sl = slice(hd * MEM_HEAD_DIM, (hd + 1) * MEM_HEAD_DIM)
        lg = _dot_nt(qmem_ref[:, sl], km_ref[:, sl]) * scale
        p = jnp.exp(lg - jnp.max(lg, axis=1, keepdims=True))
        pv = _dot(p.astype(BF16), vm_ref[:, sl])
        ymem.append((pv / jnp.sum(p, axis=1, keepdims=True)).astype(BF16))
    mix = _dot(yssd_ref[...], wo_ref[0:D_SSD, :]) + _dot(ydsa_ref[...], wo_ref[D_SSD:D_SSD + D_DSA, :])
    mix = mix + _dot(jnp.concatenate(ymem, 1), wo_ref[D_SSD + D_DSA:, :])
    h1 = _layer_norm(alpha * h_ref[...] + mix, g_ref[...], b_ref[...])
    h1_ref[...] = h1

    lt = _dot_nt(rwt_ref[...], h1.astype(BF16)) + rb_ref[:, 0:1]
    eio = lax.broadcasted_iota(I32, (N_EXPERTS, tm), 0)
    vals, idxs, hots = [], [], []
    cur = lt
    for _ in range(TOP_K):
        mx = jnp.max(cur, axis=0, keepdims=True)
        ix = jnp.min(jnp.where(cur == mx, eio, N_EXPERTS), axis=0, keepdims=True)
        hot = eio == ix
        vals.append(mx)
        idxs.append(ix)
        hots.append(hot)
        cur = jnp.where(hot, -jnp.inf, cur)
    ex = [jnp.exp(v - vals[0]) for v in vals]
    den = ex[0] + ex[1] + ex[2] + ex[3]
    multi = jnp.zeros((N_EXPERTS, tm), F32)
    for hot in hots:
        multi = multi + jnp.where(hot, 1.0, 0.0)
    ri = lax.broadcasted_iota(I32, (tm, tm), 0)
    ci = lax.broadcasted_iota(I32, (tm, tm), 1)
    before = jnp.where(ri < ci, 1.0, 0.0).astype(BF16)
    carry = carry_ref[...]
    pos = _dot(multi.astype(BF16), before) + carry[:, 0:1]
    for k in range(TOP_K):
        eidx_ref[k:k + 1, :] = idxs[k]
        gate_ref[k:k + 1, :] = ex[k] / den
        rank_ref[k:k + 1, :] = jnp.sum(jnp.where(hots[k], pos, 0.0), axis=0, keepdims=True).astype(I32)
    carry = carry + jnp.sum(multi, axis=1, keepdims=True)
    carry_ref[...] = carry
    cnt_ref[...] = carry.astype(I32)


def _mix_call(yssd, ydsa, qmem, km, vm, h, wo, g, b, rwt, rb, batch, alpha, tm=256):
    t = h.shape[0]
    spb = t // batch // tm
    m = km.shape[0] // batch
    row = lambda w: pl.BlockSpec((tm, w), lambda i: (i, 0))
    full = lambda a: pl.BlockSpec(a.shape, lambda i: (0,) * a.ndim)
    memb = pl.BlockSpec((m, D_MEMG), lambda i: (i // spb, 0))
    tok = pl.BlockSpec((TOP_K, tm), lambda i: (0, i))
    return pl.pallas_call(
        functools.partial(_mix_kernel, alpha=alpha, tm=tm),
        out_shape=[jax.ShapeDtypeStruct((t, D_MODEL), F32), jax.ShapeDtypeStruct((TOP_K, t), I32),
                   jax.ShapeDtypeStruct((TOP_K, t), F32), jax.ShapeDtypeStruct((TOP_K, t), I32),
                   jax.ShapeDtypeStruct((N_EXPERTS, LANES), I32)],
        grid=(t // tm,),
        in_specs=[row(D_SSD), row(D_DSA), row(D_MEMG), memb, memb, row(D_MODEL), full(wo), full(g), full(b),
                  full(rwt), full(rb)],
        out_specs=[row(D_MODEL), tok, tok, tok, pl.BlockSpec((N_EXPERTS, LANES), lambda i: (0, 0))],
        scratch_shapes=[pltpu.VMEM((N_EXPERTS, LANES), F32)],
        compiler_params=_params(1), name="mix_ln1_router",
    )(yssd, ydsa, qmem, km, vm, h, wo, g, b, rwt, rb)


def _dispatch_kernel(dest_ref, h_ref, xs_in, xs_hbm, sem, *, tm):
    del xs_in

    def issue(tt, _):
        for k in range(TOP_K):
            pltpu.make_async_copy(h_ref.at[pl.ds(tt, 1)], xs_hbm.at[pl.ds(dest_ref[k, tt], 1)], sem).start(priority=k % 2)
        return 0

    lax.fori_loop(0, tm, issue, 0)

    def drain(tt, _):
        for k in range(TOP_K):
            pltpu.make_async_copy(h_ref.at[pl.ds(0, 1)], xs_hbm.at[pl.ds(0, 1)], sem).wait()
        return 0

    lax.fori_loop(0, tm, drain, 0)


def _dispatch_call(dest, h1, n_rows, tm=256):
    t, d = h1.shape
    xs0 = jnp.zeros((n_rows, d), F32)
    return pl.pallas_call(
        functools.partial(_dispatch_kernel, tm=tm),
        out_shape=jax.ShapeDtypeStruct((n_rows, d), F32), grid=(t // tm,),
        in_specs=[pl.BlockSpec((TOP_K, tm), lambda i: (0, i), memory_space=pltpu.SMEM),
                  pl.BlockSpec((tm, d), lambda i: (i, 0)), pl.BlockSpec(memory_space=pl.ANY)],
        out_specs=pl.BlockSpec(memory_space=pl.ANY),
        scratch_shapes=[pltpu.SemaphoreType.DMA(())],
        input_output_aliases={2: 0},
        compiler_params=pltpu.CompilerParams(dimension_semantics=("arbitrary",), has_side_effects=True),
        name="moe_dispatch",
    )(dest, h1, xs0)


def _moe_kernel(be_ref, nu_ref, x_ref, wgu_ref, bgu_ref, wd_ref, bd_ref, y_ref):
    i = pl.program_id(0)

    @pl.when(i < nu_ref[0])
    def _():
        gu = _dot(x_ref[...].astype(BF16), wgu_ref[...]) + bgu_ref[...]
        glu = jnp.minimum(gu[:, :D_FF], SWIGLU_LIMIT)
        lin = jnp.clip(gu[:, D_FF:], -SWIGLU_LIMIT, SWIGLU_LIMIT)
        act = glu * jax.nn.sigmoid(SWIGLU_ALPHA * glu) * (lin + 1.0)
        y_ref[...] = _dot(act.astype(BF16), wd_ref[...]) + bd_ref[...]

    @pl.when(i >= nu_ref[0])
    def _():
        y_ref[...] = jnp.zeros_like(y_ref)


def _moe_call(blk_e, n_used, xs, wgu, bgu, wd, bd):
    n_rows, d = xs.shape
    n_blk = n_rows // EXPERT_BLOCK
    grid_spec = pltpu.PrefetchScalarGridSpec(
        num_scalar_prefetch=2, grid=(n_blk,),
        in_specs=[pl.BlockSpec((EXPERT_BLOCK, d), lambda i, be, nu: (i, 0)),
                  pl.BlockSpec((None, d, 2 * D_FF), lambda i, be, nu: (be[i], 0, 0)),
                  pl.BlockSpec((None, 1, 2 * D_FF), lambda i, be, nu: (be[i], 0, 0)),
                  pl.BlockSpec((None, D_FF, d), lambda i, be, nu: (be[i], 0, 0)),
                  pl.BlockSpec((None, 1, d), lambda i, be, nu: (be[i], 0, 0))],
        out_specs=pl.BlockSpec((EXPERT_BLOCK, d), lambda i, be, nu: (i, 0)))
    return pl.pallas_call(
        _moe_kernel, out_shape=jax.ShapeDtypeStruct((n_rows, d), F32), grid_spec=grid_spec,
        compiler_params=_params(1), name="moe_experts",
    )(blk_e, n_used, xs, wgu, bgu, wd, bd)


def _combine_kernel(dest_ref, ys_hbm, gate_ref, h1_ref, g_ref, b_ref, o_ref, buf_ref, sem, *, alpha, tm):
    def issue(tt, _):
        for k in range(TOP_K):
            pltpu.make_async_copy(ys_hbm.at[pl.ds(dest_ref[k, tt], 1)], buf_ref.at[k, pl.ds(tt, 1)], sem).start(priority=k % 2)
        return 0

    lax.fori_loop(0, tm, issue, 0)

    def drain(tt, _):
        for k in range(TOP_K):
            pltpu.make_async_copy(ys_hbm.at[pl.ds(0, 1)], buf_ref.at[0, pl.ds(0, 1)], sem).wait()
        return 0

    lax.fori_loop(0, tm, drain, 0)

    gt = gate_ref[...]
    ff = gt[:, 0:1] * buf_ref[0]
    for k in range(1, TOP_K):
        ff = ff + gt[:, k:k + 1] * buf_ref[k]
    o_ref[...] = _layer_norm(alpha * h1_ref[...] + ff, g_ref[...], b_ref[...])


def _combine_call(dest, ys, gates_t, h1, g, b, alpha, tm=128):
    t, d = h1.shape
    return pl.pallas_call(
        functools.partial(_combine_kernel, alpha=alpha, tm=tm),
        out_shape=jax.ShapeDtypeStruct((t, d), F32), grid=(t // tm,),
        in_specs=[pl.BlockSpec((TOP_K, tm), lambda i: (0, i), memory_space=pltpu.SMEM),
                  pl.BlockSpec(memory_space=pl.ANY),
                  pl.BlockSpec((tm, TOP_K), lambda i: (i, 0)), pl.BlockSpec((tm, d), lambda i: (i, 0)),
                  pl.BlockSpec((1, d), lambda i: (0, 0)), pl.BlockSpec((1, d), lambda i: (0, 0))],
        out_specs=pl.BlockSpec((tm, d), lambda i: (i, 0)),
        scratch_shapes=[pltpu.VMEM((TOP_K, tm, d), F32), pltpu.SemaphoreType.DMA(())],
        compiler_params=_params(1), name="moe_combine_ln2",
    )(dest, ys, gates_t, h1, g, b)


def _prep_in_weights(w_in):
    zero = lambda n: jnp.zeros(w_in.shape[:-1] + (n,), w_in.dtype)
    col = lambda o, n: w_in[..., o:o + n]
    qidx = []
    for hd in range(IDX_HEADS):
        qidx += [col(_O_QIDX + hd * IDX_DIM, IDX_DIM), zero(LANES - IDX_DIM)]
    wzx = col(_O_Z, W_ZX)
    wq = jnp.concatenate([col(_O_QLAT, DSA_HEADS * D_LAT)] + qidx +
                         [col(_O_QMEM, D_MEMG), col(_O_KIDX, IDX_DIM), zero(LANES - IDX_DIM)], -1)
    wm = jnp.concatenate([col(_O_CKV, D_LAT), col(_O_DT, SSD_HEADS), col(_O_WIDX, IDX_HEADS),
                          zero(LANES - SSD_HEADS - IDX_HEADS)], -1)
    return wzx.astype(BF16), wq.astype(BF16), wm.astype(BF16)


def _pad_lanes(a):
    return jnp.concatenate([a, jnp.zeros(a.shape[:-1] + (LANES - a.shape[-1],), a.dtype)], -1)


def kernel(x, mem, ln_in_g, ln_in_b, w_in, conv_w, conv_b, dt_bias, a_log, d_skip, ssd_norm_g, kv_norm_g, w_uv,
           w_mem_k, w_mem_v, w_out, ln1_g, ln1_b, router_w, router_b, w_gu, b_gu, w_down, b_down, ln2_g, ln2_b):
    batch, seq, d = x.shape
    depth = w_in.shape[0]
    t = batch * seq
    alpha = (2 * depth) ** 0.25
    n_slots = t * TOP_K
    n_blk = -(-n_slots // EXPERT_BLOCK) + N_EXPERTS
    n_rows = n_blk * EXPERT_BLOCK

    wzx, wq, wm = _prep_in_weights(w_in)
    e64 = (jnp.arange(LANES)[:, None] == jnp.arange(D_SSD)[None, :] // SSD_HEAD_DIM).astype(F32)
    dskip_full = jnp.repeat(d_skip, SSD_HEAD_DIM, axis=-1)
    eye = jnp.eye(DSA_HEADS, dtype=w_uv.dtype)
    wuv = (w_uv[:, :, :, None, :] * eye[None, :, None, :, None]).reshape(depth, DSA_HEADS, D_LAT, D_DSA).astype(BF16)
    mem2 = mem.reshape(batch * mem.shape[1], d)
    wmk, wmv, wo = w_mem_k.astype(BF16), w_mem_v.astype(BF16), w_out.astype(BF16)
    rwt = jnp.swapaxes(router_w, 1, 2).astype(BF16)
    rb = jnp.broadcast_to(router_b[:, :, None], (depth, N_EXPERTS, LANES))
    wgu, wd = w_gu.astype(BF16), w_down.astype(BF16)

    h = _ln_call(x.reshape(t, d), ln_in_g, ln_in_b)
    for l in range(depth):
        zx, qlat, qidx, qmem, kidx, c, dtw = _inproj_call(h, wzx[l], wq[l], wm[l], kv_norm_g[l].reshape(1, D_LAT))
        y_ssd = _ssd_call(zx, dtw, conv_w[l], conv_b[l].reshape(1, CONV_DIM), _pad_lanes(dt_bias[l].reshape(1, -1)),
                          _pad_lanes(a_log[l].reshape(1, -1)), dskip_full[l].reshape(1, D_SSD),
                          ssd_norm_g[l].reshape(1, D_SSD), e64, batch)
        y_dsa = _dsa_call(qlat, qidx, dtw, c, kidx, wuv[l], batch)
        km, vm = _memkv_call(mem2, wmk[l], wmv[l], batch)
        h1, eidx, gates, rank, counts = _mix_call(y_ssd, y_dsa, qmem, km, vm, h, wo[l], ln1_g[l].reshape(1, d),
                                                  ln1_b[l].reshape(1, d), rwt[l], rb[l], batch, alpha)
        cnt = counts[:, 0]
        padded = (cnt + EXPERT_BLOCK - 1) // EXPERT_BLOCK * EXPERT_BLOCK
        pad_end = jnp.cumsum(padded)
        e_ids = jnp.arange(N_EXPERTS, dtype=I32)
        seg_start = jnp.sum(jnp.where(eidx[:, :, None] == e_ids, pad_end - padded, 0), -1)
        dest = (seg_start + rank).astype(I32)
        blk_start = jnp.arange(n_blk, dtype=I32) * EXPERT_BLOCK
        blk_e = jnp.minimum(jnp.sum((pad_end[None, :] <= blk_start[:, None]).astype(I32), -1), N_EXPERTS - 1)
        n_used = (pad_end[-1:] // EXPERT_BLOCK).astype(I32)
        xs = _dispatch_call(dest, h1, n_rows)
        ys = _moe_call(blk_e, n_used, xs, wgu[l], b_gu[l].reshape(N_EXPERTS, 1, 2 * D_FF), wd[l],
                       b_down[l].reshape(N_EXPERTS, 1, d))
        h = _combine_call(dest, ys, gates.T, h1, ln2_g[l].reshape(1, d), ln2_b[l].reshape(1, d), alpha)
    return h.reshape(batch, seq, d)
```

```python
import functools

import jax
import jax.numpy as jnp
from jax import lax
from jax.experimental import pallas as pl
from jax.experimental.pallas import tpu as pltpu

F32, BF16, I32 = jnp.float32, jnp.bfloat16, jnp.int32
HIGHEST = lax.Precision.HIGHEST

D_MODEL = 1024
SSD_HEADS, SSD_HEAD_DIM, SSD_GROUPS, D_STATE, CONV_W, SSD_CHUNK = 16, 64, 2, 128, 4, 128
D_SSD = SSD_HEADS * SSD_HEAD_DIM
CONV_DIM = D_SSD + 2 * SSD_GROUPS * D_STATE
DSA_HEADS, DSA_HEAD_DIM, D_LAT, IDX_HEADS, IDX_DIM, TOPK_MAX = 8, 64, 128, 4, 64, 256
D_DSA = DSA_HEADS * DSA_HEAD_DIM
MEM_HEADS, MEM_HEAD_DIM = 4, 128
D_MEMG = MEM_HEADS * MEM_HEAD_DIM
D_MIX = D_SSD + D_DSA + D_MEMG
N_EXPERTS, TOP_K, D_FF = 32, 4, 1024
SWIGLU_LIMIT, SWIGLU_ALPHA = 7.0, 1.702
EXPERT_BLOCK = 256
DSA_SUB, DSA_SUB_LOG2 = 4, 2
LN_EPS, RMS_EPS = 1e-5, 1e-6

LANES = 128
SUBLANES = 8
VMEM_LIMIT = 56 * 1024 * 1024
NEG_BIG = -1e30

_O_Z, _O_XBC, _O_DT = 0, D_SSD, D_SSD + CONV_DIM
_O_QLAT = _O_DT + SSD_HEADS
_O_CKV = _O_QLAT + DSA_HEADS * D_LAT
_O_QIDX = _O_CKV + D_LAT
_O_KIDX = _O_QIDX + IDX_HEADS * IDX_DIM
_O_WIDX = _O_KIDX + IDX_DIM
_O_QMEM = _O_WIDX + IDX_HEADS
N_IN = _O_QMEM + D_MEMG
W_ZX = D_SSD + CONV_DIM
W_Q = DSA_HEADS * D_LAT + IDX_HEADS * LANES + D_MEMG + LANES
W_MISC = 2 * LANES
DTW_WIDX = SSD_HEADS


def _params(n_axes):
    return pltpu.CompilerParams(dimension_semantics=("arbitrary",) * n_axes, vmem_limit_bytes=VMEM_LIMIT)


def _layer_norm(x, g, b):
    mu = jnp.mean(x, -1, keepdims=True)
    xc = x - mu
    var = jnp.mean(xc * xc, -1, keepdims=True)
    return xc * lax.rsqrt(var + LN_EPS) * g + b


def _dot(a, b):
    return jnp.dot(a, b, preferred_element_type=F32)


def _dot_nt(a, b):
    return lax.dot_general(a, b, (((1,), (1,)), ((), ())), preferred_element_type=F32)


def _ln_kernel(x_ref, g_ref, b_ref, o_ref):
    o_ref[...] = _layer_norm(x_ref[...], g_ref[...], b_ref[...])


def _ln_call(x, g, b, tm=512):
    t, d = x.shape
    return pl.pallas_call(
        _ln_kernel, out_shape=jax.ShapeDtypeStruct((t, d), F32), grid=(t // tm,),
        in_specs=[pl.BlockSpec((tm, d), lambda i: (i, 0)), pl.BlockSpec((1, d), lambda i: (0, 0)),
                  pl.BlockSpec((1, d), lambda i: (0, 0))],
        out_specs=pl.BlockSpec((tm, d), lambda i: (i, 0)), compiler_params=_params(1), name="ln_in",
    )(x, g.reshape(1, d), b.reshape(1, d))


def _inproj_kernel(h_ref, wzx_ref, wq_ref, wm_ref, kvg_ref,
                   zx_ref, qlat_ref, qidx_ref, qmem_ref, kidx_ref, c_ref, dtw_ref):
    hb = h_ref[...].astype(BF16)
    zx_ref[...] = _dot(hb, wzx_ref[...])
    q = _dot(hb, wq_ref[...])
    o1 = DSA_HEADS * D_LAT
    o2 = o1 + IDX_HEADS * LANES
    o3 = o2 + D_MEMG
    qlat_ref[...] = q[:, :o1].astype(BF16)
    qidx_ref[...] = q[:, o1:o2].astype(BF16)
    qmem_ref[...] = q[:, o2:o3].astype(BF16)
    kidx_ref[...] = q[:, o3:].astype(BF16)
    misc = _dot(hb, wm_ref[...])
    ckv = misc[:, :D_LAT]
    c = ckv * lax.rsqrt(jnp.mean(ckv * ckv, -1, keepdims=True) + RMS_EPS) * kvg_ref[...]
    c_ref[...] = c.astype(BF16)
    dtw_ref[...] = misc[:, D_LAT:]


def _inproj_call(h, wzx, wq, wm, kvg, tm=256):
    t = h.shape[0]
    row = lambda w: pl.BlockSpec((tm, w), lambda i: (i, 0))
    full = lambda a: pl.BlockSpec(a.shape, lambda i: (0,) * a.ndim)
    outs = [(W_ZX, F32), (DSA_HEADS * D_LAT, BF16), (IDX_HEADS * LANES, BF16), (D_MEMG, BF16),
            (LANES, BF16), (D_LAT, BF16), (LANES, F32)]
    return pl.pallas_call(
        _inproj_kernel,
        out_shape=[jax.ShapeDtypeStruct((t, w), dt) for w, dt in outs],
        grid=(t // tm,),
        in_specs=[row(D_MODEL), full(wzx), full(wq), full(wm), full(kvg)],
        out_specs=[row(w) for w, _ in outs],
        compiler_params=_params(1), name="in_proj",
    )(h, wzx, wq, wm, kvg)


def _ssd_kernel(zx_ref, dtw_ref, cw_ref, cb_ref, dtb_ref, alog_ref, dskip_ref, ng_ref, e64_ref,
                y_ref, st_ref, ext_ref):
    q = SSD_CHUNK
    halo = SUBLANES

    @pl.when(pl.program_id(1) == 0)
    def _():
        st_ref[...] = jnp.zeros_like(st_ref)
        ext_ref[0:halo, :] = jnp.zeros((halo, CONV_DIM), F32)

    ext_ref[halo:halo + q, :] = zx_ref[:, D_SSD:]
    conv = cb_ref[...] + cw_ref[0:1, :] * ext_ref[halo - 3:halo - 3 + q, :]
    for j in range(1, CONV_W):
        conv = conv + cw_ref[j:j + 1, :] * ext_ref[halo - 3 + j:halo - 3 + j + q, :]
    ext_ref[0:halo, :] = ext_ref[q:q + halo, :]
    xbc = conv * jax.nn.sigmoid(conv)
    xs = xbc[:, :D_SSD]

    lane = lax.broadcasted_iota(I32, (q, LANES), 1)
    head_lane = lane < SSD_HEADS
    dtr = dtw_ref[...] + dtb_ref[...]
    dt = jnp.where(head_lane, jnp.maximum(dtr, 0.0) + jnp.log(1.0 + jnp.exp(-jnp.abs(dtr))), 0.0)
    adt = jnp.where(head_lane, dt * -jnp.exp(alog_ref[...]), 0.0)
    ri = lax.broadcasted_iota(I32, (q, q), 0)
    ci = lax.broadcasted_iota(I32, (q, q), 1)
    causal = ri >= ci
    acum = jnp.dot(causal.astype(F32), adt, precision=HIGHEST, preferred_element_type=F32)
    full = jnp.dot(jnp.concatenate([dt, acum], 0), e64_ref[...], precision=HIGHEST, preferred_element_type=F32)
    dt_full, acum_full = full[:q], full[q:]
    alast = acum_full[q - 1:q, :]
    eacum = jnp.exp(acum_full)
    dend = jnp.exp(alast - acum_full)
    dlast = jnp.exp(alast)
    xdt = xs * dt_full
    acum_t = acum.T
    lo_half = lane < SSD_HEAD_DIM

    ys = []
    hpg = SSD_HEADS // SSD_GROUPS
    for g in range(SSD_GROUPS):
        bg = xbc[:, D_SSD + g * D_STATE:D_SSD + (g + 1) * D_STATE]
        cg = xbc[:, D_SSD + (SSD_GROUPS + g) * D_STATE:D_SSD + (SSD_GROUPS + g + 1) * D_STATE].astype(BF16)
        bgt = bg.T.astype(BF16)
        cbg = _dot(cg, bgt)
        for pp in range(hpg // 2):
            pair = g * (hpg // 2) + pp
            sl = slice(pair * LANES, (pair + 1) * LANES)
            ws = []
            for hh in (2 * pair, 2 * pair + 1):
                seg = acum[:, hh:hh + 1] - acum_t[hh:hh + 1, :]
                lmat = jnp.exp(jnp.where(causal, seg, -jnp.inf))
                ws.append((cbg * lmat).astype(BF16))
            xp = xdt[:, sl]
            x_lo = jnp.where(lo_half, xp, 0.0).astype(BF16)
            x_hi = jnp.where(lo_half, 0.0, xp).astype(BF16)
            y_diag = _dot(jnp.concatenate(ws, 1), jnp.concatenate([x_lo, x_hi], 0))
            st = st_ref[pair]
            y_off = _dot(cg, st.astype(BF16)) * eacum[:, sl]
            ys.append(y_diag + y_off + dskip_ref[:, sl] * xs[:, sl])
            st_ref[pair] = st * dlast[:, sl] + _dot(bgt, (xp * dend[:, sl]).astype(BF16))
    y = jnp.concatenate(ys, 1)

    z = zx_ref[:, :D_SSD]
    hf = y * (z * jax.nn.sigmoid(z))
    gw = D_SSD // SSD_GROUPS
    outs = []
    for g in range(SSD_GROUPS):
        part = hf[:, g * gw:(g + 1) * gw]
        outs.append(part * lax.rsqrt(jnp.mean(part * part, -1, keepdims=True) + RMS_EPS))
    y_ref[...] = (jnp.concatenate(outs, 1) * ng_ref[...]).astype(BF16)


def _ssd_call(zx, dtw, cw, cb, dtb, alog, dskip, ng, e64, batch):
    t = zx.shape[0]
    q = SSD_CHUNK
    nc = t // batch // q
    blk = lambda w: pl.BlockSpec((q, w), lambda b, c: (b * nc + c, 0))
    full = lambda a: pl.BlockSpec(a.shape, lambda b, c: (0,) * a.ndim)
    return pl.pallas_call(
        _ssd_kernel, out_shape=jax.ShapeDtypeStruct((t, D_SSD), BF16), grid=(batch, nc),
        in_specs=[blk(W_ZX), blk(LANES)] + [full(a) for a in (cw, cb, dtb, alog, dskip, ng, e64)],
        out_specs=blk(D_SSD),
        scratch_shapes=[pltpu.VMEM((SSD_HEADS // 2, D_STATE, LANES), F32),
                        pltpu.VMEM((q + 2 * SUBLANES, CONV_DIM), F32)],
        compiler_params=_params(2), name="ssd",
    )(zx, dtw, cw, cb, dtb, alog, dskip, ng, e64)


def _dsa_kernel(qlat_ref, qidx_ref, dtw_ref, c_ref, kidx_ref, wuv_ref, o_ref,
                hi_ref, lo_ref, aux_ref, bias_ref, wb_ref, m_ref, l_ref, acc_ref, *, tq, n_sel):
    i = pl.program_id(1)
    nkb = i + 1
    tk = tq
    nch = lax.shift_right_logical(i + DSA_SUB, DSA_SUB_LOG2)
    half = tq // 2
    n_lt = tk // LANES
    i16_min = jnp.iinfo(jnp.int16).min
    i16_max = jnp.iinfo(jnp.int16).max

    w = dtw_ref[...]
    for h in range(IDX_HEADS):
        wb_ref[h] = jnp.broadcast_to(w[:, DTW_WIDX + h:DTW_WIDX + h + 1], (tq, LANES))

    def score_block(j, diag):
        kb = kidx_ref[pl.ds(pl.multiple_of(j * tk, tk), tk), :]
        sc = jnp.zeros((tq, tk), F32)
        for h in range(IDX_HEADS):
            lg = _dot_nt(qidx_ref[:, h * LANES:(h + 1) * LANES], kb)
            sc = sc + jnp.concatenate([wb_ref[h]] * n_lt, 1) * jnp.maximum(lg, 0.0)
        bits = pltpu.bitcast(sc, I32)
        key = bits ^ ((bits >> 31) & jnp.int32(0x7FFFFFFF))
        key = jnp.where(key == -1, 0, key)
        if diag:
            rows = lax.broadcasted_iota(I32, (tq, tk), 0)
            cols = lax.broadcasted_iota(I32, (tq, tk), 1)
            key = jnp.where(cols <= rows, key, jnp.iinfo(jnp.int32).min)
        cj, uj = lax.shift_right_logical(j, DSA_SUB_LOG2), j & (DSA_SUB - 1)
        hi_ref[cj, uj] = (key >> 16).astype(jnp.int16)
        lo_ref[cj, uj] = ((key & 0xFFFF) - 32768).astype(jnp.int16)

    def score_body(j, _):
        score_block(j, False)
        return 0

    lax.fori_loop(0, i, score_body, 0)
    score_block(i, True)

    def fill_body(j, _):
        cj, uj = lax.shift_right_logical(j, DSA_SUB_LOG2), j & (DSA_SUB - 1)
        hi_ref[cj, uj] = jnp.full((tq, tk), i16_min, jnp.int16)
        lo_ref[cj, uj] = jnp.full((tq, tk), i16_min, jnp.int16)
        return 0

    lax.fori_loop(nkb, nch * DSA_SUB, fill_body, 0)

    ones_l = jnp.ones((LANES, LANES), BF16)

    def count_ge(src_ref, cand):
        c16 = cand.astype(jnp.int16)

        def chunk(cidx, acc):
            for u in range(DSA_SUB):
                t = src_ref[cidx, u]
                for v in range(n_lt):
                    acc = acc + jnp.where(t[:, v * LANES:(v + 1) * LANES] >= c16, jnp.bfloat16(1), jnp.bfloat16(0))
            return acc

        acc = lax.fori_loop(0, nch, chunk, jnp.zeros((tq, LANES), BF16))
        return _dot(acc, ones_l)

    def bisect(src_ref, cnt0):
        def body(b, carry):
            r, cr = carry
            cand = r + jnp.left_shift(jnp.int32(1), 15 - b)
            cnt = count_ge(src_ref, cand)
            take = cnt >= n_sel
            return jnp.where(take, cand, r), jnp.where(take, cnt, cr)
        return lax.fori_loop(0, 16, body, (jnp.full((tq, LANES), i16_min, I32), cnt0))

    total = (nch * (DSA_SUB * tk)).astype(F32)
    thr_hi, cnt_hi = bisect(hi_ref, jnp.zeros((tq, LANES), F32) + total)
    hi16 = jnp.concatenate([thr_hi.astype(jnp.int16)] * n_lt, 1)

    def bucket_body(j, _):
        cj, uj = lax.shift_right_logical(j, DSA_SUB_LOG2), j & (DSA_SUB - 1)
        hv = hi_ref[cj, uj]
        aux_ref[cj, uj] = jnp.where(hv == hi16, lo_ref[cj, uj],
                                    jnp.where(hv > hi16, jnp.int16(i16_max), jnp.int16(i16_min)))
        return 0

    lax.fori_loop(0, nch * DSA_SUB, bucket_body, 0)
    thr_lo, cnt_thr = bisect(aux_ref, cnt_hi)
    lo16 = jnp.concatenate([thr_lo.astype(jnp.int16)] * n_lt, 1)

    zero16, neg16 = jnp.bfloat16(0), jnp.bfloat16(NEG_BIG)
    tied = jnp.where((cnt_thr > n_sel) & (thr_hi > i16_min), 1.0, 0.0)
    any_tied = jnp.max(tied) > 0.0

    @pl.when(jnp.logical_not(any_tied))
    def _():
        def blk(j, _):
            cj, uj = lax.shift_right_logical(j, DSA_SUB_LOG2), j & (DSA_SUB - 1)
            hv, lv = hi_ref[cj, uj], lo_ref[cj, uj]
            b16 = jnp.where(hv > hi16, zero16, jnp.where(hv == hi16, jnp.where(lv >= lo16, zero16, neg16), neg16))
            b16 = jnp.where(hv == i16_min, neg16, b16)
            bias_ref[j] = b16.astype(F32)
            return 0
        lax.fori_loop(0, nkb, blk, 0)

    @pl.when(any_tied)
    def _():
        one16 = jnp.bfloat16(1)
        ones_b = jnp.ones((tk, LANES), BF16)
        ri = lax.broadcasted_iota(I32, (tk, tk), 0)
        ci = lax.broadcasted_iota(I32, (tk, tk), 1)
        upper = jnp.where(ri <= ci, 1.0, 0.0).astype(BF16)

        def above(hv, lv):
            return jnp.where(hv > hi16, one16, jnp.where(hv == hi16, jnp.where(lv > lo16, one16, zero16), zero16))

        def cnt_blk(j, acc):
            cj, uj = lax.shift_right_logical(j, DSA_SUB_LOG2), j & (DSA_SUB - 1)
            return acc + _dot(above(hi_ref[cj, uj], lo_ref[cj, uj]), ones_b)

        need = n_sel - lax.fori_loop(0, nkb, cnt_blk, jnp.zeros((tq, LANES), F32))
        need = jnp.concatenate([need] * n_lt, 1)

        def blk(j, carry):
            cj, uj = lax.shift_right_logical(j, DSA_SUB_LOG2), j & (DSA_SUB - 1)
            hv, lv = hi_ref[cj, uj], lo_ref[cj, uj]
            gt = above(hv, lv).astype(F32)
            eq16 = jnp.where(hv == hi16, jnp.where(lv == lo16, one16, zero16), zero16)
            eq16 = jnp.where(hv == i16_min, zero16, eq16)
            pre = _dot(eq16, upper) + carry
            tie = jnp.where(eq16.astype(F32) > 0.5, jnp.where(pre <= need, 0.0, NEG_BIG), NEG_BIG)
            bias_ref[j] = jnp.where(gt > 0.5, 0.0, tie)
            return jnp.broadcast_to(pre[:, tk - 1:tk], (tq, tk))

        lax.fori_loop(0, nkb, blk, jnp.zeros((tq, tk), F32))

    m_ref[...] = jnp.full(m_ref.shape, NEG_BIG, F32)
    l_ref[...] = jnp.zeros(l_ref.shape, F32)
    acc_ref[...] = jnp.zeros(acc_ref.shape, F32)
    log2e = 1.4426950408889634
    scale2 = D_LAT ** -0.5 * log2e
    t0 = i * tq

    def attn_blk(j, _):
        cb = c_ref[pl.ds(pl.multiple_of(j * tk, tk), tk), :]
        colpos = (j * tk - t0 + lax.broadcasted_iota(I32, (1, tk), 1)).astype(F32)
        for rh in range(2):
            rows = slice(rh * half, (rh + 1) * half)
            for h in range(DSA_HEADS):
                slope2 = 2.0 ** (-8.0 * (h + 1) / DSA_HEADS) * log2e
                s = _dot_nt(qlat_ref[rows, h * D_LAT:(h + 1) * D_LAT], cb)
                x = s * scale2 + slope2 * colpos + bias_ref[j, rows, :]
                m_prev = m_ref[h, rows, :]
                m_next = jnp.maximum(m_prev, jnp.max(x, axis=1, keepdims=True))
                alpha = jnp.exp2(m_prev - m_next)
                p = jnp.exp2(x - jnp.concatenate([m_next] * n_lt, 1))
                psum = p[:, :LANES]
                for v in range(1, n_lt):
                    psum = psum + p[:, v * LANES:(v + 1) * LANES]
                l_ref[h, rows, :] = alpha * l_ref[h, rows, :] + psum
                acc_ref[h, rows, :] = alpha * acc_ref[h, rows, :] + _dot(p.astype(BF16), cb)
                m_ref[h, rows, :] = m_next
        return 0

    lax.fori_loop(0, nkb, attn_blk, 0)

    out = jnp.zeros((tq, D_DSA), F32)
    for h in range(DSA_HEADS):
        ctx = acc_ref[h] / jnp.sum(l_ref[h], axis=1, keepdims=True)
        out = out + _dot(ctx.astype(BF16), wuv_ref[h])
    o_ref[...] = out.astype(BF16)


def _dsa_call(qlat, qidx, dtw, c, kidx, wuv, batch, tq=256):
    t = qlat.shape[0]
    s = t // batch
    nq = s // tq
    nch = -(-nq // DSA_SUB)
    n_sel = min(TOPK_MAX, s // 4)
    blk = lambda w: pl.BlockSpec((tq, w), lambda b, i: (b * nq + i, 0))
    per_b = lambda w: pl.BlockSpec((s, w), lambda b, i: (b, 0))
    plane = pltpu.VMEM((nch, DSA_SUB, tq, tq), jnp.int16)
    return pl.pallas_call(
        functools.partial(_dsa_kernel, tq=tq, n_sel=n_sel),
        out_shape=jax.ShapeDtypeStruct((t, D_DSA), BF16), grid=(batch, nq),
        in_specs=[blk(DSA_HEADS * D_LAT), blk(IDX_HEADS * LANES), blk(LANES), per_b(D_LAT), per_b(LANES),
                  pl.BlockSpec(wuv.shape, lambda b, i: (0, 0, 0))],
        out_specs=blk(D_DSA),
        scratch_shapes=[plane, plane, plane, pltpu.VMEM((nq, tq, tq), F32), pltpu.VMEM((IDX_HEADS, tq, LANES), F32),
                        pltpu.VMEM((DSA_HEADS, tq, LANES), F32), pltpu.VMEM((DSA_HEADS, tq, LANES), F32),
                        pltpu.VMEM((DSA_HEADS, tq, D_LAT), F32)],
        compiler_params=_params(2), name="dsa",
    )(qlat, qidx, dtw, c, kidx, wuv)


def _memkv_kernel(mem_ref, wk_ref, wv_ref, k_ref, v_ref):
    mb = mem_ref[...].astype(BF16)
    k_ref[...] = _dot(mb, wk_ref[...]).astype(BF16)
    v_ref[...] = _dot(mb, wv_ref[...]).astype(BF16)


def _memkv_call(mem2, wk, wv, batch):
    m = mem2.shape[0] // batch
    blk = lambda w: pl.BlockSpec((m, w), lambda b: (b, 0))
    full = lambda a: pl.BlockSpec(a.shape, lambda b: (0, 0))
    return pl.pallas_call(
        _memkv_kernel, out_shape=[jax.ShapeDtypeStruct((mem2.shape[0], D_MEMG), BF16)] * 2, grid=(batch,),
        in_specs=[blk(D_MODEL), full(wk), full(wv)], out_specs=[blk(D_MEMG)] * 2,
        compiler_params=_params(1), name="mem_kv",
    )(mem2, wk, wv)


def _mix_kernel(yssd_ref, ydsa_ref, qmem_ref, km_ref, vm_ref, h_ref, wo_ref, g_ref, b_ref, rwt_ref, rb_ref,
                h1_ref, eidx_ref, gate_ref, rank_ref, cnt_ref, carry_ref, *, alpha, tm):
    @pl.when(pl.program_id(0) == 0)
    def _():
        carry_ref[...] = jnp.zeros_like(carry_ref)

    scale = MEM_HEAD_DIM ** -0.5
    ymem = []
    for hd in range(MEM_HEADS):
        sl = slice(hd * MEM_HEAD_DIM, (hd + 1) * MEM_HEAD_DIM)
        lg = _dot_nt(qmem_ref[:, sl], km_ref[:, sl]) * scale
        p = jnp.exp(lg - jnp.max(lg, axis=1, keepdims=True))
        pv = _dot(p.astype(BF16), vm_ref[:, sl])
        ymem.append((pv / jnp.sum(p, axis=1, keepdims=True)).astype(BF16))
    mix = _dot(yssd_ref[...], wo_ref[0:D_SSD, :]) + _dot(ydsa_ref[...], wo_ref[D_SSD:D_SSD + D_DSA, :])
    mix = mix + _dot(jnp.concatenate(ymem, 1), wo_ref[D_SSD + D_DSA:, :])
    h1 = _layer_norm(alpha * h_ref[...] + mix, g_ref[...], b_ref[...])
    h1_ref[...] = h1

    lt = _dot_nt(rwt_ref[...], h1.astype(BF16)) + rb_ref[:, 0:1]
    eio = lax.broadcasted_iota(I32, (N_EXPERTS, tm), 0)
    vals, idxs, hots = [], [], []
    cur = lt
    for _ in range(TOP_K):
        mx = jnp.max(cur, axis=0, keepdims=True)
        ix = jnp.min(jnp.where(cur == mx, eio, N_EXPERTS), axis=0, keepdims=True)
        hot = eio == ix
        vals.append(mx)
        idxs.append(ix)
        hots.append(hot)
        cur = jnp.where(hot, -jnp.inf, cur)
    ex = [jnp.exp(v - vals[0]) for v in vals]
    den = ex[0] + ex[1] + ex[2] + ex[3]
    multi = jnp.zeros((N_EXPERTS, tm), F32)
    for hot in hots:
        multi = multi + jnp.where(hot, 1.0, 0.0)
    ri = lax.broadcasted_iota(I32, (tm, tm), 0)
    ci = lax.broadcasted_iota(I32, (tm, tm), 1)
    before = jnp.where(ri < ci, 1.0, 0.0).astype(BF16)
    carry = carry_ref[...]
    pos = _dot(multi.astype(BF16), before) + carry[:, 0:1]
    for k in range(TOP_K):
        eidx_ref[k:k + 1, :] = idxs[k]
        gate_ref[k:k + 1, :] = ex[k] / den
        rank_ref[k:k + 1, :] = jnp.sum(jnp.where(hots[k], pos, 0.0), axis=0, keepdims=True).astype(I32)
    carry = carry + jnp.sum(multi, axis=1, keepdims=True)
    carry_ref[...] = carry
    cnt_ref[...] = carry.astype(I32)


def _mix_call(yssd, ydsa, qmem, km, vm, h, wo, g, b, rwt, rb, batch, alpha, tm=256):
    t = h.shape[0]
    spb = t // batch // tm
    m = km.shape[0] // batch
    row = lambda w: pl.BlockSpec((tm, w), lambda i: (i, 0))
    full = lambda a: pl.BlockSpec(a.shape, lambda i: (0,) * a.ndim)
    memb = pl.BlockSpec((m, D_MEMG), lambda i: (i // spb, 0))
    tok = pl.BlockSpec((TOP_K, tm), lambda i: (0, i))
    return pl.pallas_call(
        functools.partial(_mix_kernel, alpha=alpha, tm=tm),
        out_shape=[jax.ShapeDtypeStruct((t, D_MODEL), F32), jax.ShapeDtypeStruct((TOP_K, t), I32),
                   jax.ShapeDtypeStruct((TOP_K, t), F32), jax.ShapeDtypeStruct((TOP_K, t), I32),
                   jax.ShapeDtypeStruct((N_EXPERTS, LANES), I32)],
        grid=(t // tm,),
        in_specs=[row(D_SSD), row(D_DSA), row(D_MEMG), memb, memb, row(D_MODEL), full(wo), full(g), full(b),
                  full(rwt), full(rb)],
        out_specs=[row(D_MODEL), tok, tok, tok, pl.BlockSpec((N_EXPERTS, LANES), lambda i: (0, 0))],
        scratch_shapes=[pltpu.VMEM((N_EXPERTS, LANES), F32)],
        compiler_params=_params(1), name="mix_ln1_router",
    )(yssd, ydsa, qmem, km, vm, h, wo, g, b, rwt, rb)


def _dispatch_kernel(dest_ref, h_ref, xs_in, xs_hbm, sem, *, tm):
    del xs_in

    def issue(tt, _):
        for k in range(TOP_K):
            pltpu.make_async_copy(h_ref.at[pl.ds(tt, 1)], xs_hbm.at[pl.ds(dest_ref[k, tt], 1)], sem).start(priority=k % 2)
        return 0

    lax.fori_loop(0, tm, issue, 0)

    def drain(tt, _):
        for k in range(TOP_K):
            pltpu.make_async_copy(h_ref.at[pl.ds(0, 1)], xs_hbm.at[pl.ds(0, 1)], sem).wait()
        return 0

    lax.fori_loop(0, tm, drain, 0)


def _dispatch_call(dest, h1, n_rows, tm=256):
    t, d = h1.shape
    xs0 = jnp.zeros((n_rows, d), F32)
    return pl.pallas_call(
        functools.partial(_dispatch_kernel, tm=tm),
        out_shape=jax.ShapeDtypeStruct((n_rows, d), F32), grid=(t // tm,),
        in_specs=[pl.BlockSpec((TOP_K, tm), lambda i: (0, i), memory_space=pltpu.SMEM),
                  pl.BlockSpec((tm, d), lambda i: (i, 0)), pl.BlockSpec(memory_space=pl.ANY)],
        out_specs=pl.BlockSpec(memory_space=pl.ANY),
        scratch_shapes=[pltpu.SemaphoreType.DMA(())],
        input_output_aliases={2: 0},
        compiler_params=pltpu.CompilerParams(dimension_semantics=("arbitrary",), has_side_effects=True),
        name="moe_dispatch",
    )(dest, h1, xs0)


def _moe_kernel(be_ref, nu_ref, x_ref, wgu_ref, bgu_ref, wd_ref, bd_ref, y_ref):
    i = pl.program_id(0)

    @pl.when(i < nu_ref[0])
    def _():
        gu = _dot(x_ref[...].astype(BF16), wgu_ref[...]) + bgu_ref[...]
        glu = jnp.minimum(gu[:, :D_FF], SWIGLU_LIMIT)
        lin = jnp.clip(gu[:, D_FF:], -SWIGLU_LIMIT, SWIGLU_LIMIT)
        act = glu * jax.nn.sigmoid(SWIGLU_ALPHA * glu) * (lin + 1.0)
        y_ref[...] = _dot(act.astype(BF16), wd_ref[...]) + bd_ref[...]

    @pl.when(i >= nu_ref[0])
    def _():
        y_ref[...] = jnp.zeros_like(y_ref)


def _moe_call(blk_e, n_used, xs, wgu, bgu, wd, bd):
    n_rows, d = xs.shape
    n_blk = n_rows // EXPERT_BLOCK
    grid_spec = pltpu.PrefetchScalarGridSpec(
        num_scalar_prefetch=2, grid=(n_blk,),
        in_specs=[pl.BlockSpec((EXPERT_BLOCK, d), lambda i, be, nu: (i, 0)),
                  pl.BlockSpec((None, d, 2 * D_FF), lambda i, be, nu: (be[i], 0, 0)),
                  pl.BlockSpec((None, 1, 2 * D_FF), lambda i, be, nu: (be[i], 0, 0)),
                  pl.BlockSpec((None, D_FF, d), lambda i, be, nu: (be[i], 0, 0)),
                  pl.BlockSpec((None, 1, d), lambda i, be, nu: (be[i], 0, 0))],
        out_specs=pl.BlockSpec((EXPERT_BLOCK, d), lambda i, be, nu: (i, 0)))
    return pl.pallas_call(
        _moe_kernel, out_shape=jax.ShapeDtypeStruct((n_rows, d), F32), grid_spec=grid_spec,
        compiler_params=_params(1), name="moe_experts",
    )(blk_e, n_used, xs, wgu, bgu, wd, bd)


def _combine_kernel(dest_ref, ys_hbm, gate_ref, h1_ref, g_ref, b_ref, o_ref, buf_ref, sem, *, alpha, tm):
    def issue(tt, _):
        for k in range(TOP_K):
            pltpu.make_async_copy(ys_hbm.at[pl.ds(dest_ref[k, tt], 1)], buf_ref.at[k, pl.ds(tt, 1)], sem).start(priority=k % 2)
        return 0

    lax.fori_loop(0, tm, issue, 0)

    def drain(tt, _):
        for k in range(TOP_K):
            pltpu.make_async_copy(ys_hbm.at[pl.ds(0, 1)], buf_ref.at[0, pl.ds(0, 1)], sem).wait()
        return 0

    lax.fori_loop(0, tm, drain, 0)

    gt = gate_ref[...]
    ff = gt[:, 0:1] * buf_ref[0]
    for k in range(1, TOP_K):
        ff = ff + gt[:, k:k + 1] * buf_ref[k]
    o_ref[...] = _layer_norm(alpha * h1_ref[...] + ff, g_ref[...], b_ref[...])


def _combine_call(dest, ys, gates_t, h1, g, b, alpha, tm=128):
    t, d = h1.shape
    return pl.pallas_call(
        functools.partial(_combine_kernel, alpha=alpha, tm=tm),
        out_shape=jax.ShapeDtypeStruct((t, d), F32), grid=(t // tm,),
        in_specs=[pl.BlockSpec((TOP_K, tm), lambda i: (0, i), memory_space=pltpu.SMEM),
                  pl.BlockSpec(memory_space=pl.ANY),
                  pl.BlockSpec((tm, TOP_K), lambda i: (i, 0)), pl.BlockSpec((tm, d), lambda i: (i, 0)),
                  pl.BlockSpec((1, d), lambda i: (0, 0)), pl.BlockSpec((1, d), lambda i: (0, 0))],
        out_specs=pl.BlockSpec((tm, d), lambda i: (i, 0)),
        scratch_shapes=[pltpu.VMEM((TOP_K, tm, d), F32), pltpu.SemaphoreType.DMA(())],
        compiler_params=_params(1), name="moe_combine_ln2",
    )(dest, ys, gates_t, h1, g, b)


def _prep_in_weights(w_in):
    zero = lambda n: jnp.zeros(w_in.shape[:-1] + (n,), w_in.dtype)
    col = lambda o, n: w_in[..., o:o + n]
    qidx = []
    for hd in range(IDX_HEADS):
        qidx += [col(_O_QIDX + hd * IDX_DIM, IDX_DIM), zero(LANES - IDX_DIM)]
    wzx = col(_O_Z, W_ZX)
    wq = jnp.concatenate([col(_O_QLAT, DSA_HEADS * D_LAT)] + qidx +
                         [col(_O_QMEM, D_MEMG), col(_O_KIDX, IDX_DIM), zero(LANES - IDX_DIM)], -1)
    wm = jnp.concatenate([col(_O_CKV, D_LAT), col(_O_DT, SSD_HEADS), col(_O_WIDX, IDX_HEADS),
                          zero(LANES - SSD_HEADS - IDX_HEADS)], -1)
    return wzx.astype(BF16), wq.astype(BF16), wm.astype(BF16)


def _pad_lanes(a):
    return jnp.concatenate([a, jnp.zeros(a.shape[:-1] + (LANES - a.shape[-1],), a.dtype)], -1)


def kernel(x, mem, ln_in_g, ln_in_b, w_in, conv_w, conv_b, dt_bias, a_log, d_skip, ssd_norm_g, kv_norm_g, w_uv,
           w_mem_k, w_mem_v, w_out, ln1_g, ln1_b, router_w, router_b, w_gu, b_gu, w_down, b_down, ln2_g, ln2_b):
    batch, seq, d = x.shape
    depth = w_in.shape[0]
    t = batch * seq
    alpha = (2 * depth) ** 0.25
    n_slots = t * TOP_K
    n_blk = -(-n_slots // EXPERT_BLOCK) + N_EXPERTS
    n_rows = n_blk * EXPERT_BLOCK

    wzx, wq, wm = _prep_in_weights(w_in)
    e64 = (jnp.arange(LANES)[:, None] == jnp.arange(D_SSD)[None, :] // SSD_HEAD_DIM).astype(F32)
    dskip_full = jnp.repeat(d_skip, SSD_HEAD_DIM, axis=-1)
    eye = jnp.eye(DSA_HEADS, dtype=w_uv.dtype)
    wuv = (w_uv[:, :, :, None, :] * eye[None, :, None, :, None]).reshape(depth, DSA_HEADS, D_LAT, D_DSA).astype(BF16)
    mem2 = mem.reshape(batch * mem.shape[1], d)
    wmk, wmv, wo = w_mem_k.astype(BF16), w_mem_v.astype(BF16), w_out.astype(BF16)
    rwt = jnp.swapaxes(router_w, 1, 2).astype(BF16)
    rb = jnp.broadcast_to(router_b[:, :, None], (depth, N_EXPERTS, LANES))
    wgu, wd = w_gu.astype(BF16), w_down.astype(BF16)

    h = _ln_call(x.reshape(t, d), ln_in_g, ln_in_b)
    for l in range(depth):
        zx, qlat, qidx, qmem, kidx, c, dtw = _inproj_call(h, wzx[l], wq[l], wm[l], kv_norm_g[l].reshape(1, D_LAT))
        y_ssd = _ssd_call(zx, dtw, conv_w[l], conv_b[l].reshape(1, CONV_DIM), _pad_lanes(dt_bias[l].reshape(1, -1)),
                          _pad_lanes(a_log[l].reshape(1, -1)), dskip_full[l].reshape(1, D_SSD),
                          ssd_norm_g[l].reshape(1, D_SSD), e64, batch)
        y_dsa = _dsa_call(qlat, qidx, dtw, c, kidx, wuv[l], batch)
        km, vm = _memkv_call(mem2, wmk[l], wmv[l], batch)
        h1, eidx, gates, rank, counts = _mix_call(y_ssd, y_dsa, qmem, km, vm, h, wo[l], ln1_g[l].reshape(1, d),
                                                  ln1_b[l].reshape(1, d), rwt[l], rb[l], batch, alpha)
        cnt = counts[:, 0]
        padded = (cnt + EXPERT_BLOCK - 1) // EXPERT_BLOCK * EXPERT_BLOCK
        pad_end = jnp.cumsum(padded)
        e_ids = jnp.arange(N_EXPERTS, dtype=I32)
        seg_start = jnp.sum(jnp.where(eidx[:, :, None] == e_ids, pad_end - padded, 0), -1)
        dest = (seg_start + rank).astype(I32)
        blk_start = jnp.arange(n_blk, dtype=I32) * EXPERT_BLOCK
        blk_e = jnp.minimum(jnp.sum((pad_end[None, :] <= blk_start[:, None]).astype(I32), -1), N_EXPERTS - 1)
        n_used = (pad_end[-1:] // EXPERT_BLOCK).astype(I32)
        xs = _dispatch_call(dest, h1, n_rows)
        ys = _moe_call(blk_e, n_used, xs, wgu[l], b_gu[l].reshape(N_EXPERTS, 1, 2 * D_FF), wd[l],
                       b_down[l].reshape(N_EXPERTS, 1, d))
        h = _combine_call(dest, ys, gates.T, h1, ln2_g[l].reshape(1, d), ln2_b[l].reshape(1, d), alpha)
    return h.reshape(batch, seq, d)
```

```python
import functools

import jax
import jax.numpy as jnp
from jax import lax
from jax.experimental import pallas as pl
from jax.experimental.pallas import tpu as pltpu

F32, BF16, I32 = jnp.float32, jnp.bfloat16, jnp.int32
HIGHEST = lax.Precision.HIGHEST

D_MODEL = 1024
SSD_HEADS, SSD_HEAD_DIM, SSD_GROUPS, D_STATE, CONV_W, SSD_CHUNK = 16, 64, 2, 128, 4, 128
D_SSD = SSD_HEADS * SSD_HEAD_DIM
CONV_DIM = D_SSD + 2 * SSD_GROUPS * D_STATE
DSA_HEADS, DSA_HEAD_DIM, D_LAT, IDX_HEADS, IDX_DIM, TOPK_MAX = 8, 64, 128, 4, 64, 256
D_DSA = DSA_HEADS * DSA_HEAD_DIM
MEM_HEADS, MEM_HEAD_DIM = 4, 128
D_MEMG = MEM_HEADS * MEM_HEAD_DIM
D_MIX = D_SSD + D_DSA + D_MEMG
N_EXPERTS, TOP_K, D_FF = 32, 4, 1024
SWIGLU_LIMIT, SWIGLU_ALPHA = 7.0, 1.702
EXPERT_BLOCK = 256
LN_EPS, RMS_EPS = 1e-5, 1e-6

LANES = 128
SUBLANES = 8
VMEM_LIMIT = 56 * 1024 * 1024
NEG_BIG = -1e30

_O_Z, _O_XBC, _O_DT = 0, D_SSD, D_SSD + CONV_DIM
_O_QLAT = _O_DT + SSD_HEADS
_O_CKV = _O_QLAT + DSA_HEADS * D_LAT
_O_QIDX = _O_CKV + D_LAT
_O_KIDX = _O_QIDX + IDX_HEADS * IDX_DIM
_O_WIDX = _O_KIDX + IDX_DIM
_O_QMEM = _O_WIDX + IDX_HEADS
N_IN = _O_QMEM + D_MEMG
W_ZX = D_SSD + CONV_DIM
W_Q = DSA_HEADS * D_LAT + IDX_HEADS * LANES + D_MEMG + LANES
W_MISC = 2 * LANES
DTW_WIDX = SSD_HEADS


def _params(n_axes):
    return pltpu.CompilerParams(dimension_semantics=("arbitrary",) * n_axes, vmem_limit_bytes=VMEM_LIMIT)


def _layer_norm(x, g, b):
    mu = jnp.mean(x, -1, keepdims=True)
    xc = x - mu
    var = jnp.mean(xc * xc, -1, keepdims=True)
    return xc * lax.rsqrt(var + LN_EPS) * g + b


def _dot(a, b):
    return jnp.dot(a, b, preferred_element_type=F32)


def _dot_nt(a, b):
    return lax.dot_general(a, b, (((1,), (1,)), ((), ())), preferred_element_type=F32)


def _ln_kernel(x_ref, g_ref, b_ref, o_ref):
    o_ref[...] = _layer_norm(x_ref[...], g_ref[...], b_ref[...])


def _ln_call(x, g, b, tm=512):
    t, d = x.shape
    return pl.pallas_call(
        _ln_kernel, out_shape=jax.ShapeDtypeStruct((t, d), F32), grid=(t // tm,),
        in_specs=[pl.BlockSpec((tm, d), lambda i: (i, 0)), pl.BlockSpec((1, d), lambda i: (0, 0)),
                  pl.BlockSpec((1, d), lambda i: (0, 0))],
        out_specs=pl.BlockSpec((tm, d), lambda i: (i, 0)), compiler_params=_params(1), name="ln_in",
    )(x, g.reshape(1, d), b.reshape(1, d))


def _inproj_kernel(h_ref, wzx_ref, wq_ref, wm_ref, kvg_ref,
                   zx_ref, qlat_ref, qidx_ref, qmem_ref, kidx_ref, c_ref, dtw_ref):
    hb = h_ref[...].astype(BF16)
    zx_ref[...] = _dot(hb, wzx_ref[...])
    q = _dot(hb, wq_ref[...])
    o1 = DSA_HEADS * D_LAT
    o2 = o1 + IDX_HEADS * LANES
    o3 = o2 + D_MEMG
    qlat_ref[...] = q[:, :o1].astype(BF16)
    qidx_ref[...] = q[:, o1:o2].astype(BF16)
    qmem_ref[...] = q[:, o2:o3].astype(BF16)
    kidx_ref[...] = q[:, o3:].astype(BF16)
    misc = _dot(hb, wm_ref[...])
    ckv = misc[:, :D_LAT]
    c = ckv * lax.rsqrt(jnp.mean(ckv * ckv, -1, keepdims=True) + RMS_EPS) * kvg_ref[...]
    c_ref[...] = c.astype(BF16)
    dtw_ref[...] = misc[:, D_LAT:]


def _inproj_call(h, wzx, wq, wm, kvg, tm=256):
    t = h.shape[0]
    row = lambda w: pl.BlockSpec((tm, w), lambda i: (i, 0))
    full = lambda a: pl.BlockSpec(a.shape, lambda i: (0,) * a.ndim)
    outs = [(W_ZX, F32), (DSA_HEADS * D_LAT, BF16), (IDX_HEADS * LANES, BF16), (D_MEMG, BF16),
            (LANES, BF16), (D_LAT, BF16), (LANES, F32)]
    return pl.pallas_call(
        _inproj_kernel,
        out_shape=[jax.ShapeDtypeStruct((t, w), dt) for w, dt in outs],
        grid=(t // tm,),
        in_specs=[row(D_MODEL), full(wzx), full(wq), full(wm), full(kvg)],
        out_specs=[row(w) for w, _ in outs],
        compiler_params=_params(1), name="in_proj",
    )(h, wzx, wq, wm, kvg)


def _ssd_kernel(zx_ref, dtw_ref, cw_ref, cb_ref, dtb_ref, alog_ref, dskip_ref, ng_ref, e64_ref,
                y_ref, st_ref, ext_ref):
    q = SSD_CHUNK
    halo = SUBLANES

    @pl.when(pl.program_id(1) == 0)
    def _():
        st_ref[...] = jnp.zeros_like(st_ref)
        ext_ref[0:halo, :] = jnp.zeros((halo, CONV_DIM), F32)

    ext_ref[halo:halo + q, :] = zx_ref[:, D_SSD:]
    conv = cb_ref[...] + cw_ref[0:1, :] * ext_ref[halo - 3:halo - 3 + q, :]
    for j in range(1, CONV_W):
        conv = conv + cw_ref[j:j + 1, :] * ext_ref[halo - 3 + j:halo - 3 + j + q, :]
    ext_ref[0:halo, :] = ext_ref[q:q + halo, :]
    xbc = conv * jax.nn.sigmoid(conv)
    xs = xbc[:, :D_SSD]

    lane = lax.broadcasted_iota(I32, (q, LANES), 1)
    head_lane = lane < SSD_HEADS
    dtr = dtw_ref[...] + dtb_ref[...]
    dt = jnp.where(head_lane, jnp.maximum(dtr, 0.0) + jnp.log(1.0 + jnp.exp(-jnp.abs(dtr))), 0.0)
    adt = jnp.where(head_lane, dt * -jnp.exp(alog_ref[...]), 0.0)
    ri = lax.broadcasted_iota(I32, (q, q), 0)
    ci = lax.broadcasted_iota(I32, (q, q), 1)
    causal = ri >= ci
    acum = jnp.dot(causal.astype(F32), adt, precision=HIGHEST, preferred_element_type=F32)
    full = jnp.dot(jnp.concatenate([dt, acum], 0), e64_ref[...], precision=HIGHEST, preferred_element_type=F32)
    dt_full, acum_full = full[:q], full[q:]
    alast = acum_full[q - 1:q, :]
    eacum = jnp.exp(acum_full)
    dend = jnp.exp(alast - acum_full)
    dlast = jnp.exp(alast)
    xdt = xs * dt_full
    acum_t = acum.T
    lo_half = lane < SSD_HEAD_DIM

    ys = []
    hpg = SSD_HEADS // SSD_GROUPS
    for g in range(SSD_GROUPS):
        bg = xbc[:, D_SSD + g * D_STATE:D_SSD + (g + 1) * D_STATE]
        cg = xbc[:, D_SSD + (SSD_GROUPS + g) * D_STATE:D_SSD + (SSD_GROUPS + g + 1) * D_STATE].astype(BF16)
        bgt = bg.T.astype(BF16)
        cbg = _dot(cg, bgt)
        for pp in range(hpg // 2):
            pair = g * (hpg // 2) + pp
            sl = slice(pair * LANES, (pair + 1) * LANES)
            ws = []
            for hh in (2 * pair, 2 * pair + 1):
                seg = acum[:, hh:hh + 1] - acum_t[hh:hh + 1, :]
                lmat = jnp.exp(jnp.where(causal, seg, -jnp.inf))
                ws.append((cbg * lmat).astype(BF16))
            xp = xdt[:, sl]
            x_lo = jnp.where(lo_half, xp, 0.0).astype(BF16)
            x_hi = jnp.where(lo_half, 0.0, xp).astype(BF16)
            y_diag = _dot(jnp.concatenate(ws, 1), jnp.concatenate([x_lo, x_hi], 0))
            st = st_ref[pair]
            y_off = _dot(cg, st.astype(BF16)) * eacum[:, sl]
            ys.append(y_diag + y_off + dskip_ref[:, sl] * xs[:, sl])
            st_ref[pair] = st * dlast[:, sl] + _dot(bgt, (xp * dend[:, sl]).astype(BF16))
    y = jnp.concatenate(ys, 1)

    z = zx_ref[:, :D_SSD]
    hf = y * (z * jax.nn.sigmoid(z))
    gw = D_SSD // SSD_GROUPS
    outs = []
    for g in range(SSD_GROUPS):
        part = hf[:, g * gw:(g + 1) * gw]
        outs.append(part * lax.rsqrt(jnp.mean(part * part, -1, keepdims=True) + RMS_EPS))
    y_ref[...] = (jnp.concatenate(outs, 1) * ng_ref[...]).astype(BF16)


def _ssd_call(zx, dtw, cw, cb, dtb, alog, dskip, ng, e64, batch):
    t = zx.shape[0]
    q = SSD_CHUNK
    nc = t // batch // q
    blk = lambda w: pl.BlockSpec((q, w), lambda b, c: (b * nc + c, 0))
    full = lambda a: pl.BlockSpec(a.shape, lambda b, c: (0,) * a.ndim)
    return pl.pallas_call(
        _ssd_kernel, out_shape=jax.ShapeDtypeStruct((t, D_SSD), BF16), grid=(batch, nc),
        in_specs=[blk(W_ZX), blk(LANES)] + [full(a) for a in (cw, cb, dtb, alog, dskip, ng, e64)],
        out_specs=blk(D_SSD),
        scratch_shapes=[pltpu.VMEM((SSD_HEADS // 2, D_STATE, LANES), F32),
                        pltpu.VMEM((q + 2 * SUBLANES, CONV_DIM), F32)],
        compiler_params=_params(2), name="ssd",
    )(zx, dtw, cw, cb, dtb, alog, dskip, ng, e64)


def _dsa_kernel(qlat_ref, qidx_ref, dtw_ref, c_ref, kidx_ref, wuv_ref, o_ref,
                key_ref, thr_ref, cnt_ref, wb_ref, m_ref, l_ref, acc_ref, *, tq, n_sel):
    i = pl.program_id(1)
    nkb = i + 1
    tk = tq
    half = tq // 2
    n_lt = tk // LANES
    i32_min = jnp.iinfo(jnp.int32).min

    w = dtw_ref[...]
    for h in range(IDX_HEADS):
        wb_ref[h] = jnp.broadcast_to(w[:, DTW_WIDX + h:DTW_WIDX + h + 1], (tq, LANES))

    def score_block(j, diag):
        kb = kidx_ref[pl.ds(pl.multiple_of(j * tk, tk), tk), :]
        sc = jnp.zeros((tq, tk), F32)
        for h in range(IDX_HEADS):
            lg = _dot_nt(qidx_ref[:, h * LANES:(h + 1) * LANES], kb)
            sc = sc + jnp.concatenate([wb_ref[h]] * n_lt, 1) * jnp.maximum(lg, 0.0)
        bits = pltpu.bitcast(sc, I32)
        key = bits ^ ((bits >> 31) & jnp.int32(0x7FFFFFFF))
        key = jnp.where(key == -1, 0, key)
        if diag:
            rows = lax.broadcasted_iota(I32, (tq, tk), 0)
            cols = lax.broadcasted_iota(I32, (tq, tk), 1)
            key = jnp.where(cols <= rows, key, i32_min)
        key_ref[j] = key

    def score_body(j, _):
        score_block(j, False)
        return 0

    lax.fori_loop(0, i, score_body, 0)
    score_block(i, True)

    def counts_ge(offset):
        accs = []
        for g in range(2):
            rows = slice(g * half, (g + 1) * half)
            cand = thr_ref[rows, :] + offset

            def blk(j, acc, rows=rows, cand=cand):
                t = key_ref[j, rows, :]
                for v in range(n_lt):
                    acc = acc + jnp.where(t[:, v * LANES:(v + 1) * LANES] >= cand, 1.0, 0.0)
                return acc

            accs.append(lax.fori_loop(0, nkb, blk, jnp.zeros((half, LANES), F32)))
        return [jnp.broadcast_to(jnp.sum(a, axis=1, keepdims=True), (half, LANES)) for a in accs]

    thr_ref[...] = jnp.full((tq, LANES), i32_min, I32)
    cnt_ref[...] = jnp.zeros((tq, LANES), F32) + (nkb * tk).astype(F32)

    def bit_body(b, _):
        step = jnp.left_shift(jnp.int32(1), 31 - b)
        cnts = counts_ge(step)
        for g in range(2):
            rows = slice(g * half, (g + 1) * half)
            take = cnts[g] >= n_sel
            thr_ref[rows, :] = jnp.where(take, thr_ref[rows, :] + step, thr_ref[rows, :])
            cnt_ref[rows, :] = jnp.where(take, cnts[g], cnt_ref[rows, :])
        return 0

    lax.fori_loop(0, 32, bit_body, 0)

    thr = thr_ref[...]
    tied = jnp.where((cnt_ref[...] > n_sel) & (thr > i32_min), 1.0, 0.0)
    any_tied = jnp.max(tied) > 0.0

    @pl.when(jnp.logical_not(any_tied))
    def _():
        floor = jnp.concatenate([jnp.maximum(thr, i32_min + 1)] * n_lt, 1)

        def blk(j, _):
            key_ref[j] = pltpu.bitcast(jnp.where(key_ref[j] >= floor, 0.0, NEG_BIG), I32)
            return 0
        lax.fori_loop(0, nkb, blk, 0)

    @pl.when(any_tied)
    def _():
        cnts = counts_ge(jnp.int32(1))
        need = n_sel - jnp.concatenate(cnts, 0)
        need = jnp.concatenate([need] * n_lt, 1)
        r = jnp.concatenate([thr] * n_lt, 1)
        ri = lax.broadcasted_iota(I32, (tk, tk), 0)
        ci = lax.broadcasted_iota(I32, (tk, tk), 1)
        upper = jnp.where(ri <= ci, 1.0, 0.0).astype(BF16)

        def blk(j, carry):
            t = key_ref[j]
            eq = (t == r) & (t > i32_min)
            pre = _dot(jnp.where(eq, 1.0, 0.0).astype(BF16), upper) + carry
            tie = jnp.where(eq, jnp.where(pre <= need, 0.0, NEG_BIG), NEG_BIG)
            key_ref[j] = pltpu.bitcast(jnp.where(t > r, 0.0, tie), I32)
            return jnp.broadcast_to(pre[:, tk - 1:tk], (tq, tk))

        lax.fori_loop(0, nkb, blk, jnp.zeros((tq, tk), F32))

    m_ref[...] = jnp.full(m_ref.shape, NEG_BIG, F32)
    l_ref[...] = jnp.zeros(l_ref.shape, F32)
    acc_ref[...] = jnp.zeros(acc_ref.shape, F32)
    log2e = 1.4426950408889634
    scale2 = D_LAT ** -0.5 * log2e
    t0 = i * tq

    def attn_blk(j, _):
        cb = c_ref[pl.ds(pl.multiple_of(j * tk, tk), tk), :]
        colpos = (j * tk - t0 + lax.broadcasted_iota(I32, (1, tk), 1)).astype(F32)
        for rh in range(2):
            rows = slice(rh * half, (rh + 1) * half)
            for h in range(DSA_HEADS):
                slope2 = 2.0 ** (-8.0 * (h + 1) / DSA_HEADS) * log2e
                s = _dot_nt(qlat_ref[rows, h * D_LAT:(h + 1) * D_LAT], cb)
                x = s * scale2 + slope2 * colpos + pltpu.bitcast(key_ref[j, rows, :], F32)
                m_prev = m_ref[h, rows, :]
                m_next = jnp.maximum(m_prev, jnp.max(x, axis=1, keepdims=True))
                alpha = jnp.exp2(m_prev - m_next)
                p = jnp.exp2(x - jnp.concatenate([m_next] * n_lt, 1))
                psum = p[:, :LANES]
                for v in range(1, n_lt):
                    psum = psum + p[:, v * LANES:(v + 1) * LANES]
                l_ref[h, rows, :] = alpha * l_ref[h, rows, :] + psum
                acc_ref[h, rows, :] = alpha * acc_ref[h, rows, :] + _dot(p.astype(BF16), cb)
                m_ref[h, rows, :] = m_next
        return 0

    lax.fori_loop(0, nkb, attn_blk, 0)

    out = jnp.zeros((tq, D_DSA), F32)
    for h in range(DSA_HEADS):
        ctx = acc_ref[h] / jnp.sum(l_ref[h], axis=1, keepdims=True)
        out = out + _dot(ctx.astype(BF16), wuv_ref[h])
    o_ref[...] = out.astype(BF16)


def _dsa_call(qlat, qidx, dtw, c, kidx, wuv, batch, tq=256):
    t = qlat.shape[0]
    s = t // batch
    nq = s // tq
    n_sel = min(TOPK_MAX, s // 4)
    blk = lambda w: pl.BlockSpec((tq, w), lambda b, i: (b * nq + i, 0))
    per_b = lambda w: pl.BlockSpec((s, w), lambda b, i: (b, 0))
    return pl.pallas_call(
        functools.partial(_dsa_kernel, tq=tq, n_sel=n_sel),
        out_shape=jax.ShapeDtypeStruct((t, D_DSA), BF16), grid=(batch, nq),
        in_specs=[blk(DSA_HEADS * D_LAT), blk(IDX_HEADS * LANES), blk(LANES), per_b(D_LAT), per_b(LANES),
                  pl.BlockSpec(wuv.shape, lambda b, i: (0, 0, 0))],
        out_specs=blk(D_DSA),
        scratch_shapes=[pltpu.VMEM((nq, tq, tq), I32), pltpu.VMEM((tq, LANES), I32), pltpu.VMEM((tq, LANES), F32),
                        pltpu.VMEM((IDX_HEADS, tq, LANES), F32),
                        pltpu.VMEM((DSA_HEADS, tq, LANES), F32), pltpu.VMEM((DSA_HEADS, tq, LANES), F32),
                        pltpu.VMEM((DSA_HEADS, tq, D_LAT), F32)],
        compiler_params=_params(2), name="dsa",
    )(qlat, qidx, dtw, c, kidx, wuv)


def _memkv_kernel(mem_ref, wk_ref, wv_ref, k_ref, v_ref):
    mb = mem_ref[...].astype(BF16)
    k_ref[...] = _dot(mb, wk_ref[...]).astype(BF16)
    v_ref[...] = _dot(mb, wv_ref[...]).astype(BF16)


def _memkv_call(mem2, wk, wv, batch):
    m = mem2.shape[0] // batch
    blk = lambda w: pl.BlockSpec((m, w), lambda b: (b, 0))
    full = lambda a: pl.BlockSpec(a.shape, lambda b: (0, 0))
    return pl.pallas_call(
        _memkv_kernel, out_shape=[jax.ShapeDtypeStruct((mem2.shape[0], D_MEMG), BF16)] * 2, grid=(batch,),
        in_specs=[blk(D_MODEL), full(wk), full(wv)], out_specs=[blk(D_MEMG)] * 2,
        compiler_params=_params(1), name="mem_kv",
    )(mem2, wk, wv)


def _mix_kernel(yssd_ref, ydsa_ref, qmem_ref, km_ref, vm_ref, h_ref, wo_ref, g_ref, b_ref, rwt_ref, rb_ref,
                h1_ref, eidx_ref, gate_ref, rank_ref, cnt_ref, carry_ref, *, alpha, tm):
    @pl.when(pl.program_id(0) == 0)
    def _():
        carry_ref[...] = jnp.zeros_like(carry_ref)

    scale = MEM_HEAD_DIM ** -0.5
    ymem = []
    for hd in range(MEM_HEADS):
        sl = slice(hd * MEM_HEAD_DIM, (hd + 1) * MEM_HEAD_DIM)
        lg = _dot_nt(qmem_ref[:, sl], km_ref[:, sl]) * scale
        p = jnp.exp(lg - jnp.max(lg, axis=1, keepdims=True))
        pv = _dot(p.astype(BF16), vm_ref[:, sl])
        ymem.append((pv / jnp.sum(p, axis=1, keepdims=True)).astype(BF16))
    mix = _dot(yssd_ref[...], wo_ref[0:D_SSD, :]) + _dot(ydsa_ref[...], wo_ref[D_SSD:D_SSD + D_DSA, :])
    mix = mix + _dot(jnp.concatenate(ymem, 1), wo_ref[D_SSD + D_DSA:, :])
    h1 = _layer_norm(alpha * h_ref[...] + mix, g_ref[...], b_ref[...])
    h1_ref[...] = h1

    lt = _dot_nt(rwt_ref[...], h1.astype(BF16)) + rb_ref[:, 0:1]
    eio = lax.broadcasted_iota(I32, (N_EXPERTS, tm), 0)
    vals, idxs, hots = [], [], []
    cur = lt
    for _ in range(TOP_K):
        mx = jnp.max(cur, axis=0, keepdims=True)
        ix = jnp.min(jnp.where(cur == mx, eio, N_EXPERTS), axis=0, keepdims=True)
        hot = eio == ix
        vals.append(mx)
        idxs.append(ix)
        hots.append(hot)
        cur = jnp.where(hot, -jnp.inf, cur)
    ex = [jnp.exp(v - vals[0]) for v in vals]
    den = ex[0] + ex[1] + ex[2] + ex[3]
    multi = jnp.zeros((N_EXPERTS, tm), F32)
    for hot in hots:
        multi = multi + jnp.where(hot, 1.0, 0.0)
    ri = lax.broadcasted_iota(I32, (tm, tm), 0)
    ci = lax.broadcasted_iota(I32, (tm, tm), 1)
    before = jnp.where(ri < ci, 1.0, 0.0).astype(BF16)
    carry = carry_ref[...]
    pos = _dot(multi.astype(BF16), before) + carry[:, 0:1]
    for k in range(TOP_K):
        eidx_ref[k:k + 1, :] = idxs[k]
        gate_ref[k:k + 1, :] = ex[k] / den
        rank_ref[k:k + 1, :] = jnp.sum(jnp.where(hots[k], pos, 0.0), axis=0, keepdims=True).astype(I32)
    carry = carry + jnp.sum(multi, axis=1, keepdims=True)
    carry_ref[...] = carry
    cnt_ref[...] = carry.astype(I32)


def _mix_call(yssd, ydsa, qmem, km, vm, h, wo, g, b, rwt, rb, batch, alpha, tm=256):
    t = h.shape[0]
    spb = t // batch // tm
    m = km.shape[0] // batch
    row = lambda w: pl.BlockSpec((tm, w), lambda i: (i, 0))
    full = lambda a: pl.BlockSpec(a.shape, lambda i: (0,) * a.ndim)
    memb = pl.BlockSpec((m, D_MEMG), lambda i: (i // spb, 0))
    tok = pl.BlockSpec((TOP_K, tm), lambda i: (0, i))
    return pl.pallas_call(
        functools.partial(_mix_kernel, alpha=alpha, tm=tm),
        out_shape=[jax.ShapeDtypeStruct((t, D_MODEL), F32), jax.ShapeDtypeStruct((TOP_K, t), I32),
                   jax.ShapeDtypeStruct((TOP_K, t), F32), jax.ShapeDtypeStruct((TOP_K, t), I32),
                   jax.ShapeDtypeStruct((N_EXPERTS, LANES), I32)],
        grid=(t // tm,),
        in_specs=[row(D_SSD), row(D_DSA), row(D_MEMG), memb, memb, row(D_MODEL), full(wo), full(g), full(b),
                  full(rwt), full(rb)],
        out_specs=[row(D_MODEL), tok, tok, tok, pl.BlockSpec((N_EXPERTS, LANES), lambda i: (0, 0))],
        scratch_shapes=[pltpu.VMEM((N_EXPERTS, LANES), F32)],
        compiler_params=_params(1), name="mix_ln1_router",
    )(yssd, ydsa, qmem, km, vm, h, wo, g, b, rwt, rb)


def _dispatch_kernel(dest_ref, h_ref, xs_in, xs_hbm, sem, *, tm):
    del xs_in

    def issue(tt, _):
        for k in range(TOP_K):
            pltpu.make_async_copy(h_ref.at[pl.ds(tt, 1)], xs_hbm.at[pl.ds(dest_ref[k, tt], 1)], sem).start(priority=k % 2)
        return 0

    lax.fori_loop(0, tm, issue, 0)

    def drain(tt, _):
        for k in range(TOP_K):
            pltpu.make_async_copy(h_ref.at[pl.ds(0, 1)], xs_hbm.at[pl.ds(0, 1)], sem).wait()
        return 0

    lax.fori_loop(0, tm, drain, 0)


def _dispatch_call(dest, h1, n_rows, tm=256):
    t, d = h1.shape
    xs0 = jnp.zeros((n_rows, d), F32)
    return pl.pallas_call(
        functools.partial(_dispatch_kernel, tm=tm),
        out_shape=jax.ShapeDtypeStruct((n_rows, d), F32), grid=(t // tm,),
        in_specs=[pl.BlockSpec((TOP_K, tm), lambda i: (0, i), memory_space=pltpu.SMEM),
                  pl.BlockSpec((tm, d), lambda i: (i, 0)), pl.BlockSpec(memory_space=pl.ANY)],
        out_specs=pl.BlockSpec(memory_space=pl.ANY),
        scratch_shapes=[pltpu.SemaphoreType.DMA(())],
        input_output_aliases={2: 0},
        compiler_params=pltpu.CompilerParams(dimension_semantics=("arbitrary",), has_side_effects=True),
        name="moe_dispatch",
    )(dest, h1, xs0)


def _moe_kernel(be_ref, nu_ref, x_ref, wgu_ref, bgu_ref, wd_ref, bd_ref, y_ref):
    i = pl.program_id(0)

    @pl.when(i < nu_ref[0])
    def _():
        gu = _dot(x_ref[...].astype(BF16), wgu_ref[...]) + bgu_ref[...]
        glu = jnp.minimum(gu[:, :D_FF], SWIGLU_LIMIT)
        lin = jnp.clip(gu[:, D_FF:], -SWIGLU_LIMIT, SWIGLU_LIMIT)
        act = glu * jax.nn.sigmoid(SWIGLU_ALPHA * glu) * (lin + 1.0)
        y_ref[...] = _dot(act.astype(BF16), wd_ref[...]) + bd_ref[...]

    @pl.when(i >= nu_ref[0])
    def _():
        y_ref[...] = jnp.zeros_like(y_ref)


def _moe_call(blk_e, n_used, xs, wgu, bgu, wd, bd):
    n_rows, d = xs.shape
    n_blk = n_rows // EXPERT_BLOCK
    grid_spec = pltpu.PrefetchScalarGridSpec(
        num_scalar_prefetch=2, grid=(n_blk,),
        in_specs=[pl.BlockSpec((EXPERT_BLOCK, d), lambda i, be, nu: (i, 0)),
                  pl.BlockSpec((None, d, 2 * D_FF), lambda i, be, nu: (be[i], 0, 0)),
                  pl.BlockSpec((None, 1, 2 * D_FF), lambda i, be, nu: (be[i], 0, 0)),
                  pl.BlockSpec((None, D_FF, d), lambda i, be, nu: (be[i], 0, 0)),
                  pl.BlockSpec((None, 1, d), lambda i, be, nu: (be[i], 0, 0))],
        out_specs=pl.BlockSpec((EXPERT_BLOCK, d), lambda i, be, nu: (i, 0)))
    return pl.pallas_call(
        _moe_kernel, out_shape=jax.ShapeDtypeStruct((n_rows, d), F32), grid_spec=grid_spec,
        compiler_params=_params(1), name="moe_experts",
    )(blk_e, n_used, xs, wgu, bgu, wd, bd)


def _combine_kernel(dest_ref, ys_hbm, gate_ref, h1_ref, g_ref, b_ref, o_ref, buf_ref, sem, *, alpha, tm):
    def issue(tt, _):
        for k in range(TOP_K):
            pltpu.make_async_copy(ys_hbm.at[pl.ds(dest_ref[k, tt], 1)], buf_ref.at[k, pl.ds(tt, 1)], sem).start(priority=k % 2)
        return 0

    lax.fori_loop(0, tm, issue, 0)

    def drain(tt, _):
        for k in range(TOP_K):
            pltpu.make_async_copy(ys_hbm.at[pl.ds(0, 1)], buf_ref.at[0, pl.ds(0, 1)], sem).wait()
        return 0

    lax.fori_loop(0, tm, drain, 0)

    gt = gate_ref[...]
    ff = gt[:, 0:1] * buf_ref[0]
    for k in range(1, TOP_K):
        ff = ff + gt[:, k:k + 1] * buf_ref[k]
    o_ref[...] = _layer_norm(alpha * h1_ref[...] + ff, g_ref[...], b_ref[...])


def _combine_call(dest, ys, gates_t, h1, g, b, alpha, tm=128):
    t, d = h1.shape
    return pl.pallas_call(
        functools.partial(_combine_kernel, alpha=alpha, tm=tm),
        out_shape=jax.ShapeDtypeStruct((t, d), F32), grid=(t // tm,),
        in_specs=[pl.BlockSpec((TOP_K, tm), lambda i: (0, i), memory_space=pltpu.SMEM),
                  pl.BlockSpec(memory_space=pl.ANY),
                  pl.BlockSpec((tm, TOP_K), lambda i: (i, 0)), pl.BlockSpec((tm, d), lambda i: (i, 0)),
                  pl.BlockSpec((1, d), lambda i: (0, 0)), pl.BlockSpec((1, d), lambda i: (0, 0))],
        out_specs=pl.BlockSpec((tm, d), lambda i: (i, 0)),
        scratch_shapes=[pltpu.VMEM((TOP_K, tm, d), F32), pltpu.SemaphoreType.DMA(())],
        compiler_params=_params(1), name="moe_combine_ln2",
    )(dest, ys, gates_t, h1, g, b)


def _prep_in_weights(w_in):
    zero = lambda n: jnp.zeros(w_in.shape[:-1] + (n,), w_in.dtype)
    col = lambda o, n: w_in[..., o:o + n]
    qidx = []
    for hd in range(IDX_HEADS):
        qidx += [col(_O_QIDX + hd * IDX_DIM, IDX_DIM), zero(LANES - IDX_DIM)]
    wzx = col(_O_Z, W_ZX)
    wq = jnp.concatenate([col(_O_QLAT, DSA_HEADS * D_LAT)] + qidx +
                         [col(_O_QMEM, D_MEMG), col(_O_KIDX, IDX_DIM), zero(LANES - IDX_DIM)], -1)
    wm = jnp.concatenate([col(_O_CKV, D_LAT), col(_O_DT, SSD_HEADS), col(_O_WIDX, IDX_HEADS),
                          zero(LANES - SSD_HEADS - IDX_HEADS)], -1)
    return wzx.astype(BF16), wq.astype(BF16), wm.astype(BF16)


def _pad_lanes(a):
    return jnp.concatenate([a, jnp.zeros(a.shape[:-1] + (LANES - a.shape[-1],), a.dtype)], -1)


def kernel(x, mem, ln_in_g, ln_in_b, w_in, conv_w, conv_b, dt_bias, a_log, d_skip, ssd_norm_g, kv_norm_g, w_uv,
           w_mem_k, w_mem_v, w_out, ln1_g, ln1_b, router_w, router_b, w_gu, b_gu, w_down, b_down, ln2_g, ln2_b):
    batch, seq, d = x.shape
    depth = w_in.shape[0]
    t = batch * seq
    alpha = (2 * depth) ** 0.25
    n_slots = t * TOP_K
    n_blk = -(-n_slots // EXPERT_BLOCK) + N_EXPERTS
    n_rows = n_blk * EXPERT_BLOCK

    wzx, wq, wm = _prep_in_weights(w_in)
    e64 = (jnp.arange(LANES)[:, None] == jnp.arange(D_SSD)[None, :] // SSD_HEAD_DIM).astype(F32)
    dskip_full = jnp.repeat(d_skip, SSD_HEAD_DIM, axis=-1)
    eye = jnp.eye(DSA_HEADS, dtype=w_uv.dtype)
    wuv = (w_uv[:, :, :, None, :] * eye[None, :, None, :, None]).reshape(depth, DSA_HEADS, D_LAT, D_DSA).astype(BF16)
    mem2 = mem.reshape(batch * mem.shape[1], d)
    wmk, wmv, wo = w_mem_k.astype(BF16), w_mem_v.astype(BF16), w_out.astype(BF16)
    rwt = jnp.swapaxes(router_w, 1, 2).astype(BF16)
    rb = jnp.broadcast_to(router_b[:, :, None], (depth, N_EXPERTS, LANES))
    wgu, wd = w_gu.astype(BF16), w_down.astype(BF16)

    h = _ln_call(x.reshape(t, d), ln_in_g, ln_in_b)
    for l in range(depth):
        zx, qlat, qidx, qmem, kidx, c, dtw = _inproj_call(h, wzx[l], wq[l], wm[l], kv_norm_g[l].reshape(1, D_LAT))
        y_ssd = _ssd_call(zx, dtw, conv_w[l], conv_b[l].reshape(1, CONV_DIM), _pad_lanes(dt_bias[l].reshape(1, -1)),
                          _pad_lanes(a_log[l].reshape(1, -1)), dskip_full[l].reshape(1, D_SSD),
                          ssd_norm_g[l].reshape(1, D_SSD), e64, batch)
        y_dsa = _dsa_call(qlat, qidx, dtw, c, kidx, wuv[l], batch)
        km, vm = _memkv_call(mem2, wmk[l], wmv[l], batch)
        h1, eidx, gates, rank, counts = _mix_call(y_ssd, y_dsa, qmem, km, vm, h, wo[l], ln1_g[l].reshape(1, d),
                                                  ln1_b[l].reshape(1, d), rwt[l], rb[l], batch, alpha)
        cnt = counts[:, 0]
        padded = (cnt + EXPERT_BLOCK - 1) // EXPERT_BLOCK * EXPERT_BLOCK
        pad_end = jnp.cumsum(padded)
        e_ids = jnp.arange(N_EXPERTS, dtype=I32)
        seg_start = jnp.sum(jnp.where(eidx[:, :, None] == e_ids, pad_end - padded, 0), -1)
        dest = (seg_start + rank).astype(I32)
        blk_start = jnp.arange(n_blk, dtype=I32) * EXPERT_BLOCK
        blk_e = jnp.minimum(jnp.sum((pad_end[None, :] <= blk_start[:, None]).astype(I32), -1), N_EXPERTS - 1)
        n_used = (pad_end[-1:] // EXPERT_BLOCK).astype(I32)
        xs = _dispatch_call(dest, h1, n_rows)
        ys = _moe_call(blk_e, n_used, xs, wgu[l], b_gu[l].reshape(N_EXPERTS, 1, 2 * D_FF), wd[l],
                       b_down[l].reshape(N_EXPERTS, 1, d))
        h = _combine_call(dest, ys, gates.T, h1, ln2_g[l].reshape(1, d), ln2_b[l].reshape(1, d), alpha)
    return h.reshape(batch, seq, d)
```

```python
import functools

import jax
import jax.numpy as jnp
from jax import lax
from jax.experimental import pallas as pl
from jax.experimental.pallas import tpu as pltpu

F32, BF16, I32 = jnp.float32, jnp.bfloat16, jnp.int32
HIGHEST = lax.Precision.HIGHEST

D_MODEL = 1024
SSD_HEADS, SSD_HEAD_DIM, SSD_GROUPS, D_STATE, CONV_W, SSD_CHUNK = 16, 64, 2, 128, 4, 128
D_SSD = SSD_HEADS * SSD_HEAD_DIM
CONV_DIM = D_SSD + 2 * SSD_GROUPS * D_STATE
DSA_HEADS, DSA_HEAD_DIM, D_LAT, IDX_HEADS, IDX_DIM, TOPK_MAX = 8, 64, 128, 4, 64, 256
D_DSA = DSA_HEADS * DSA_HEAD_DIM
MEM_HEADS, MEM_HEAD_DIM = 4, 128
D_MEMG = MEM_HEADS * MEM_HEAD_DIM
D_MIX = D_SSD + D_DSA + D_MEMG
N_EXPERTS, TOP_K, D_FF = 32, 4, 1024
SWIGLU_LIMIT, SWIGLU_ALPHA = 7.0, 1.702
EXPERT_BLOCK = 256
LN_EPS, RMS_EPS = 1e-5, 1e-6

LANES = 128
SUBLANES = 8
VMEM_LIMIT = 56 * 1024 * 1024
NEG_BIG = -1e30

_O_Z, _O_XBC, _O_DT = 0, D_SSD, D_SSD + CONV_DIM
_O_QLAT = _O_DT + SSD_HEADS
_O_CKV = _O_QLAT + DSA_HEADS * D_LAT
_O_QIDX = _O_CKV + D_LAT
_O_KIDX = _O_QIDX + IDX_HEADS * IDX_DIM
_O_WIDX = _O_KIDX + IDX_DIM
_O_QMEM = _O_WIDX + IDX_HEADS
N_IN = _O_QMEM + D_MEMG
W_ZX = D_SSD + CONV_DIM
W_Q = DSA_HEADS * D_LAT + IDX_HEADS * LANES + D_MEMG + LANES
W_MISC = 2 * LANES
DTW_WIDX = SSD_HEADS


def _params(n_axes):
    return pltpu.CompilerParams(dimension_semantics=("arbitrary",) * n_axes, vmem_limit_bytes=VMEM_LIMIT)


def _layer_norm(x, g, b):
    mu = jnp.mean(x, -1, keepdims=True)
    xc = x - mu
    var = jnp.mean(xc * xc, -1, keepdims=True)
    return xc * lax.rsqrt(var + LN_EPS) * g + b


def _dot(a, b):
    return jnp.dot(a, b, preferred_element_type=F32)


def _dot_nt(a, b):
    return lax.dot_general(a, b, (((1,), (1,)), ((), ())), preferred_element_type=F32)


def _ln_kernel(x_ref, g_ref, b_ref, o_ref):
    o_ref[...] = _layer_norm(x_ref[...], g_ref[...], b_ref[...])


def _ln_call(x, g, b, tm=512):
    t, d = x.shape
    return pl.pallas_call(
        _ln_kernel, out_shape=jax.ShapeDtypeStruct((t, d), F32), grid=(t // tm,),
        in_specs=[pl.BlockSpec((tm, d), lambda i: (i, 0)), pl.BlockSpec((1, d), lambda i: (0, 0)),
                  pl.BlockSpec((1, d), lambda i: (0, 0))],
        out_specs=pl.BlockSpec((tm, d), lambda i: (i, 0)), compiler_params=_params(1), name="ln_in",
    )(x, g.reshape(1, d), b.reshape(1, d))


def _inproj_kernel(h_ref, wzx_ref, wq_ref, wm_ref, kvg_ref,
                   zx_ref, qlat_ref, qidx_ref, qmem_ref, kidx_ref, c_ref, dtw_ref):
    hb = h_ref[...].astype(BF16)
    zx_ref[...] = _dot(hb, wzx_ref[...])
    q = _dot(hb, wq_ref[...])
    o1 = DSA_HEADS * D_LAT
    o2 = o1 + IDX_HEADS * LANES
    o3 = o2 + D_MEMG
    qlat_ref[...] = q[:, :o1].astype(BF16)
    qidx_ref[...] = q[:, o1:o2].astype(BF16)
    qmem_ref[...] = q[:, o2:o3].astype(BF16)
    kidx_ref[...] = q[:, o3:].astype(BF16)
    misc = _dot(hb, wm_ref[...])
    ckv = misc[:, :D_LAT]
    c = ckv * lax.rsqrt(jnp.mean(ckv * ckv, -1, keepdims=True) + RMS_EPS) * kvg_ref[...]
    c_ref[...] = c.astype(BF16)
    dtw_ref[...] = misc[:, D_LAT:]


def _inproj_call(h, wzx, wq, wm, kvg, tm=256):
    t = h.shape[0]
    row = lambda w: pl.BlockSpec((tm, w), lambda i: (i, 0))
    full = lambda a: pl.BlockSpec(a.shape, lambda i: (0,) * a.ndim)
    outs = [(W_ZX, F32), (DSA_HEADS * D_LAT, BF16), (IDX_HEADS * LANES, BF16), (D_MEMG, BF16),
            (LANES, BF16), (D_LAT, BF16), (LANES, F32)]
    return pl.pallas_call(
        _inproj_kernel,
        out_shape=[jax.ShapeDtypeStruct((t, w), dt) for w, dt in outs],
        grid=(t // tm,),
        in_specs=[row(D_MODEL), full(wzx), full(wq), full(wm), full(kvg)],
        out_specs=[row(w) for w, _ in outs],
        compiler_params=_params(1), name="in_proj",
    )(h, wzx, wq, wm, kvg)


def _ssd_kernel(zx_ref, dtw_ref, cw_ref, cb_ref, dtb_ref, alog_ref, dskip_ref, ng_ref, e64_ref,
                y_ref, st_ref, ext_ref):
    q = SSD_CHUNK
    halo = SUBLANES

    @pl.when(pl.program_id(1) == 0)
    def _():
        st_ref[...] = jnp.zeros_like(st_ref)
        ext_ref[0:halo, :] = jnp.zeros((halo, CONV_DIM), F32)

    ext_ref[halo:halo + q, :] = zx_ref[:, D_SSD:]
    conv = cb_ref[...] + cw_ref[0:1, :] * ext_ref[halo - 3:halo - 3 + q, :]
    for j in range(1, CONV_W):
        conv = conv + cw_ref[j:j + 1, :] * ext_ref[halo - 3 + j:halo - 3 + j + q, :]
    ext_ref[0:halo, :] = ext_ref[q:q + halo, :]
    xbc = conv * jax.nn.sigmoid(conv)
    xs = xbc[:, :D_SSD]

    lane = lax.broadcasted_iota(I32, (q, LANES), 1)
    head_lane = lane < SSD_HEADS
    dtr = dtw_ref[...] + dtb_ref[...]
    dt = jnp.where(head_lane, jnp.maximum(dtr, 0.0) + jnp.log(1.0 + jnp.exp(-jnp.abs(dtr))), 0.0)
    adt = jnp.where(head_lane, dt * -jnp.exp(alog_ref[...]), 0.0)
    ri = lax.broadcasted_iota(I32, (q, q), 0)
    ci = lax.broadcasted_iota(I32, (q, q), 1)
    causal = ri >= ci
    acum = jnp.dot(causal.astype(F32), adt, precision=HIGHEST, preferred_element_type=F32)
    both = jnp.concatenate([dt, acum], 0)
    p1 = both.astype(BF16)
    r1 = both - p1.astype(F32)
    p2 = r1.astype(BF16)
    p3 = (r1 - p2.astype(F32)).astype(BF16)
    full = _dot(jnp.concatenate([p1, p2, p3], 1), e64_ref[...])
    dt_full, acum_full = full[:q], full[q:]
    alast = acum_full[q - 1:q, :]
    eacum = jnp.exp(acum_full)
    dend = jnp.exp(alast - acum_full)
    dlast = jnp.exp(alast)
    xdt = xs * dt_full
    acum_t = acum.T
    lo_half = lane < SSD_HEAD_DIM

    ys = []
    hpg = SSD_HEADS // SSD_GROUPS
    for g in range(SSD_GROUPS):
        bg = xbc[:, D_SSD + g * D_STATE:D_SSD + (g + 1) * D_STATE]
        cg = xbc[:, D_SSD + (SSD_GROUPS + g) * D_STATE:D_SSD + (SSD_GROUPS + g + 1) * D_STATE].astype(BF16)
        bgt = bg.T.astype(BF16)
        cbg = _dot(cg, bgt)
        for pp in range(hpg // 2):
            pair = g * (hpg // 2) + pp
            sl = slice(pair * LANES, (pair + 1) * LANES)
            ws = []
            for hh in (2 * pair, 2 * pair + 1):
                seg = acum[:, hh:hh + 1] - acum_t[hh:hh + 1, :]
                lmat = jnp.exp(jnp.where(causal, seg, -jnp.inf))
                ws.append((cbg * lmat).astype(BF16))
            xp = xdt[:, sl]
            x_lo = jnp.where(lo_half, xp, 0.0).astype(BF16)
            x_hi = jnp.where(lo_half, 0.0, xp).astype(BF16)
            y_diag = _dot(jnp.concatenate(ws, 1), jnp.concatenate([x_lo, x_hi], 0))
            st = st_ref[pair]
            y_off = _dot(cg, st.astype(BF16)) * eacum[:, sl]
            ys.append(y_diag + y_off + dskip_ref[:, sl] * xs[:, sl])
            st_ref[pair] = st * dlast[:, sl] + _dot(bgt, (xp * dend[:, sl]).astype(BF16))
    y = jnp.concatenate(ys, 1)

    z = zx_ref[:, :D_SSD]
    hf = y * (z * jax.nn.sigmoid(z))
    gw = D_SSD // SSD_GROUPS
    outs = []
    for g in range(SSD_GROUPS):
        part = hf[:, g * gw:(g + 1) * gw]
        outs.append(part * lax.rsqrt(jnp.mean(part * part, -1, keepdims=True) + RMS_EPS))
    y_ref[...] = (jnp.concatenate(outs, 1) * ng_ref[...]).astype(BF16)


def _ssd_call(zx, dtw, cw, cb, dtb, alog, dskip, ng, e64, batch):
    t = zx.shape[0]
    q = SSD_CHUNK
    nc = t // batch // q
    blk = lambda w: pl.BlockSpec((q, w), lambda b, c: (b * nc + c, 0))
    full = lambda a: pl.BlockSpec(a.shape, lambda b, c: (0,) * a.ndim)
    return pl.pallas_call(
        _ssd_kernel, out_shape=jax.ShapeDtypeStruct((t, D_SSD), BF16), grid=(batch, nc),
        in_specs=[blk(W_ZX), blk(LANES)] + [full(a) for a in (cw, cb, dtb, alog, dskip, ng, e64)],
        out_specs=blk(D_SSD),
        scratch_shapes=[pltpu.VMEM((SSD_HEADS // 2, D_STATE, LANES), F32),
                        pltpu.VMEM((q + 2 * SUBLANES, CONV_DIM), F32)],
        compiler_params=_params(2), name="ssd",
    )(zx, dtw, cw, cb, dtb, alog, dskip, ng, e64)


def _dsa_kernel(qlat_ref, qidx_ref, dtw_ref, c_ref, kidx_ref, wuv_ref, o_ref,
                key_ref, thr_ref, cnt_ref, wb_ref, m_ref, l_ref, acc_ref, *, tq, n_sel):
    i = pl.program_id(1)
    nkb = i + 1
    tk = tq
    half = tq // 2
    n_lt = tk // LANES
    i32_min = jnp.iinfo(jnp.int32).min

    w = dtw_ref[...]
    for h in range(IDX_HEADS):
        wb_ref[h] = jnp.broadcast_to(w[:, DTW_WIDX + h:DTW_WIDX + h + 1], (tq, LANES))

    def score_block(j, diag):
        kb = kidx_ref[pl.ds(pl.multiple_of(j * tk, tk), tk), :]
        sc = jnp.zeros((tq, tk), F32)
        for h in range(IDX_HEADS):
            lg = _dot_nt(qidx_ref[:, h * LANES:(h + 1) * LANES], kb)
            sc = sc + jnp.concatenate([wb_ref[h]] * n_lt, 1) * jnp.maximum(lg, 0.0)
        bits = pltpu.bitcast(sc, I32)
        key = bits ^ ((bits >> 31) & jnp.int32(0x7FFFFFFF))
        key = jnp.where(key == -1, 0, key)
        if diag:
            rows = lax.broadcasted_iota(I32, (tq, tk), 0)
            cols = lax.broadcasted_iota(I32, (tq, tk), 1)
            key = jnp.where(cols <= rows, key, i32_min)
        key_ref[j] = key

    def score_body(j, _):
        score_block(j, False)
        return 0

    lax.fori_loop(0, i, score_body, 0)
    score_block(i, True)

    def counts_ge(offset):
        accs = []
        for g in range(2):
            rows = slice(g * half, (g + 1) * half)
            cand = thr_ref[rows, :] + offset

            def blk(j, acc, rows=rows, cand=cand):
                t = key_ref[j, rows, :]
                for v in range(n_lt):
                    acc = acc + jnp.where(t[:, v * LANES:(v + 1) * LANES] >= cand, 1.0, 0.0)
                return acc

            accs.append(lax.fori_loop(0, nkb, blk, jnp.zeros((half, LANES), F32)))
        return [jnp.broadcast_to(jnp.sum(a, axis=1, keepdims=True), (half, LANES)) for a in accs]

    thr_ref[...] = jnp.full((tq, LANES), i32_min, I32)
    cnt_ref[...] = jnp.zeros((tq, LANES), F32) + (nkb * tk).astype(F32)

    def bit_body(b, _):
        step = jnp.left_shift(jnp.int32(1), 31 - b)
        cnts = counts_ge(step)
        for g in range(2):
            rows = slice(g * half, (g + 1) * half)
            take = cnts[g] >= n_sel
            thr_ref[rows, :] = jnp.where(take, thr_ref[rows, :] + step, thr_ref[rows, :])
            cnt_ref[rows, :] = jnp.where(take, cnts[g], cnt_ref[rows, :])
        return 0

    lax.fori_loop(0, 32, bit_body, 0)

    thr = thr_ref[...]
    tied = jnp.where((cnt_ref[...] > n_sel) & (thr > i32_min), 1.0, 0.0)
    any_tied = jnp.max(tied) > 0.0

    @pl.when(jnp.logical_not(any_tied))
    def _():
        floor = jnp.concatenate([jnp.maximum(thr, i32_min + 1)] * n_lt, 1)

        def blk(j, _):
            key_ref[j] = pltpu.bitcast(jnp.where(key_ref[j] >= floor, 0.0, NEG_BIG), I32)
            return 0
        lax.fori_loop(0, nkb, blk, 0)

    @pl.when(any_tied)
    def _():
        cnts = counts_ge(jnp.int32(1))
        need = n_sel - jnp.concatenate(cnts, 0)
        need = jnp.concatenate([need] * n_lt, 1)
        r = jnp.concatenate([thr] * n_lt, 1)
        ri = lax.broadcasted_iota(I32, (tk, tk), 0)
        ci = lax.broadcasted_iota(I32, (tk, tk), 1)
        upper = jnp.where(ri <= ci, 1.0, 0.0).astype(BF16)

        def blk(j, carry):
            t = key_ref[j]
            eq = (t == r) & (t > i32_min)
            pre = _dot(jnp.where(eq, 1.0, 0.0).astype(BF16), upper) + carry
            tie = jnp.where(eq, jnp.where(pre <= need, 0.0, NEG_BIG), NEG_BIG)
            key_ref[j] = pltpu.bitcast(jnp.where(t > r, 0.0, tie), I32)
            return jnp.broadcast_to(pre[:, tk - 1:tk], (tq, tk))

        lax.fori_loop(0, nkb, blk, jnp.zeros((tq, tk), F32))

    m_ref[...] = jnp.full(m_ref.shape, NEG_BIG, F32)
    l_ref[...] = jnp.zeros(l_ref.shape, F32)
    acc_ref[...] = jnp.zeros(acc_ref.shape, F32)
    log2e = 1.4426950408889634
    scale2 = D_LAT ** -0.5 * log2e
    t0 = i * tq

    def attn_blk(j, _):
        cb = c_ref[pl.ds(pl.multiple_of(j * tk, tk), tk), :]
        colpos = (j * tk - t0 + lax.broadcasted_iota(I32, (1, tk), 1)).astype(F32)
        for rh in range(2):
            rows = slice(rh * half, (rh + 1) * half)
            for h in range(DSA_HEADS):
                slope2 = 2.0 ** (-8.0 * (h + 1) / DSA_HEADS) * log2e
                s = _dot_nt(qlat_ref[rows, h * D_LAT:(h + 1) * D_LAT], cb)
                x = s * scale2 + slope2 * colpos + pltpu.bitcast(key_ref[j, rows, :], F32)
                m_prev = m_ref[h, rows, :]
                m_next = jnp.maximum(m_prev, jnp.max(x, axis=1, keepdims=True))
                alpha = jnp.exp2(m_prev - m_next)
                p = jnp.exp2(x - jnp.concatenate([m_next] * n_lt, 1))
                psum = p[:, :LANES]
                for v in range(1, n_lt):
                    psum = psum + p[:, v * LANES:(v + 1) * LANES]
                l_ref[h, rows, :] = alpha * l_ref[h, rows, :] + psum
                acc_ref[h, rows, :] = alpha * acc_ref[h, rows, :] + _dot(p.astype(BF16), cb)
                m_ref[h, rows, :] = m_next
        return 0

    lax.fori_loop(0, nkb, attn_blk, 0)

    out = jnp.zeros((tq, D_DSA), F32)
    for h in range(DSA_HEADS):
        ctx = acc_ref[h] / jnp.sum(l_ref[h], axis=1, keepdims=True)
        out = out + _dot(ctx.astype(BF16), wuv_ref[h])
    o_ref[...] = out.astype(BF16)


def _dsa_call(qlat, qidx, dtw, c, kidx, wuv, batch, tq=256):
    t = qlat.shape[0]
    s = t // batch
    nq = s // tq
    n_sel = min(TOPK_MAX, s // 4)
    blk = lambda w: pl.BlockSpec((tq, w), lambda b, i: (b * nq + i, 0))
    per_b = lambda w: pl.BlockSpec((s, w), lambda b, i: (b, 0))
    return pl.pallas_call(
        functools.partial(_dsa_kernel, tq=tq, n_sel=n_sel),
        out_shape=jax.ShapeDtypeStruct((t, D_DSA), BF16), grid=(batch, nq),
        in_specs=[blk(DSA_HEADS * D_LAT), blk(IDX_HEADS * LANES), blk(LANES), per_b(D_LAT), per_b(LANES),
                  pl.BlockSpec(wuv.shape, lambda b, i: (0, 0, 0))],
        out_specs=blk(D_DSA),
        scratch_shapes=[pltpu.VMEM((nq, tq, tq), I32), pltpu.VMEM((tq, LANES), I32), pltpu.VMEM((tq, LANES), F32),
                        pltpu.VMEM((IDX_HEADS, tq, LANES), F32),
                        pltpu.VMEM((DSA_HEADS, tq, LANES), F32), pltpu.VMEM((DSA_HEADS, tq, LANES), F32),
                        pltpu.VMEM((DSA_HEADS, tq, D_LAT), F32)],
        compiler_params=_params(2), name="dsa",
    )(qlat, qidx, dtw, c, kidx, wuv)


def _memkv_kernel(mem_ref, wk_ref, wv_ref, k_ref, v_ref):
    mb = mem_ref[...].astype(BF16)
    k_ref[...] = _dot(mb, wk_ref[...]).astype(BF16)
    v_ref[...] = _dot(mb, wv_ref[...]).astype(BF16)


def _memkv_call(mem2, wk, wv, batch):
    m = mem2.shape[0] // batch
    blk = lambda w: pl.BlockSpec((m, w), lambda b: (b, 0))
    full = lambda a: pl.BlockSpec(a.shape, lambda b: (0, 0))
    return pl.pallas_call(
        _memkv_kernel, out_shape=[jax.ShapeDtypeStruct((mem2.shape[0], D_MEMG), BF16)] * 2, grid=(batch,),
        in_specs=[blk(D_MODEL), full(wk), full(wv)], out_specs=[blk(D_MEMG)] * 2,
        compiler_params=_params(1), name="mem_kv",
    )(mem2, wk, wv)


def _mix_kernel(yssd_ref, ydsa_ref, qmem_ref, km_ref, vm_ref, h_ref, wo_ref, g_ref, b_ref, rwt_ref, rb_ref,
                h1_ref, eidx_ref, gate_ref, rank_ref, cnt_ref, carry_ref, *, alpha, tm):
    @pl.when(pl.program_id(0) == 0)
    def _():
        carry_ref[...] = jnp.zeros_like(carry_ref)

    scale = MEM_HEAD_DIM ** -0.5
    ymem = []
    for hd in range(MEM_HEADS):
        sl = slice(hd * MEM_HEAD_DIM, (hd + 1) * MEM_HEAD_DIM)
        lg = _dot_nt(qmem_ref[:, sl], km_ref[:, sl]) * scale
        p = jnp.exp(lg - jnp.max(lg, axis=1, keepdims=True))
        pv = _dot(p.astype(BF16), vm_ref[:, sl])
        ymem.append((pv / jnp.sum(p, axis=1, keepdims=True)).astype(BF16))
    mix = _dot(yssd_ref[...], wo_ref[0:D_SSD, :]) + _dot(ydsa_ref[...], wo_ref[D_SSD:D_SSD + D_DSA, :])
    mix = mix + _dot(jnp.concatenate(ymem, 1), wo_ref[D_SSD + D_DSA:, :])
    h1 = _layer_norm(alpha * h_ref[...] + mix, g_ref[...], b_ref[...])
    h1_ref[...] = h1

    lt = _dot_nt(rwt_ref[...], h1.astype(BF16)) + rb_ref[:, 0:1]
    eio = lax.broadcasted_iota(I32, (N_EXPERTS, tm), 0)
    vals, idxs, hots = [], [], []
    cur = lt
    for _ in range(TOP_K):
        mx = jnp.max(cur, axis=0, keepdims=True)
        ix = jnp.min(jnp.where(cur == mx, eio, N_EXPERTS), axis=0, keepdims=True)
        hot = eio == ix
        vals.append(mx)
        idxs.append(ix)
        hots.append(hot)
        cur = jnp.where(hot, -jnp.inf, cur)
    ex = [jnp.exp(v - vals[0]) for v in vals]
    den = ex[0] + ex[1] + ex[2] + ex[3]
    multi = jnp.zeros((N_EXPERTS, tm), F32)
    for hot in hots:
        multi = multi + jnp.where(hot, 1.0, 0.0)
    ri = lax.broadcasted_iota(I32, (tm, tm), 0)
    ci = lax.broadcasted_iota(I32, (tm, tm), 1)
    before = jnp.where(ri < ci, 1.0, 0.0).astype(BF16)
    carry = carry_ref[...]
    pos = _dot(multi.astype(BF16), before) + carry[:, 0:1]
    for k in range(TOP_K):
        eidx_ref[k:k + 1, :] = idxs[k]
        gate_ref[k:k + 1, :] = ex[k] / den
        rank_ref[k:k + 1, :] = jnp.sum(jnp.where(hots[k], pos, 0.0), axis=0, keepdims=True).astype(I32)
    carry = carry + jnp.sum(multi, axis=1, keepdims=True)
    carry_ref[...] = carry
    cnt_ref[...] = carry.astype(I32)


def _mix_call(yssd, ydsa, qmem, km, vm, h, wo, g, b, rwt, rb, batch, alpha, tm=256):
    t = h.shape[0]
    spb = t // batch // tm
    m = km.shape[0] // batch
    row = lambda w: pl.BlockSpec((tm, w), lambda i: (i, 0))
    full = lambda a: pl.BlockSpec(a.shape, lambda i: (0,) * a.ndim)
    memb = pl.BlockSpec((m, D_MEMG), lambda i: (i // spb, 0))
    tok = pl.BlockSpec((TOP_K, tm), lambda i: (0, i))
    return pl.pallas_call(
        functools.partial(_mix_kernel, alpha=alpha, tm=tm),
        out_shape=[jax.ShapeDtypeStruct((t, D_MODEL), F32), jax.ShapeDtypeStruct((TOP_K, t), I32),
                   jax.ShapeDtypeStruct((TOP_K, t), F32), jax.ShapeDtypeStruct((TOP_K, t), I32),
                   jax.ShapeDtypeStruct((N_EXPERTS, LANES), I32)],
        grid=(t // tm,),
        in_specs=[row(D_SSD), row(D_DSA), row(D_MEMG), memb, memb, row(D_MODEL), full(wo), full(g), full(b),
                  full(rwt), full(rb)],
        out_specs=[row(D_MODEL), tok, tok, tok, pl.BlockSpec((N_EXPERTS, LANES), lambda i: (0, 0))],
        scratch_shapes=[pltpu.VMEM((N_EXPERTS, LANES), F32)],
        compiler_params=_params(1), name="mix_ln1_router",
    )(yssd, ydsa, qmem, km, vm, h, wo, g, b, rwt, rb)


def _dispatch_kernel(dest_ref, h_ref, xs_in, xs_hbm, sem, *, tm):
    del xs_in

    def issue(t8, _):
        for r in range(SUBLANES):
            tt = t8 * SUBLANES + r
            for k in range(TOP_K):
                pltpu.make_async_copy(h_ref.at[pl.ds(tt, 1)], xs_hbm.at[pl.ds(dest_ref[k, tt], 1)],
                                      sem).start(priority=k % 2)
        return 0

    lax.fori_loop(0, tm // SUBLANES, issue, 0)

    def drain(tt, _):
        for k in range(TOP_K):
            pltpu.make_async_copy(h_ref.at[pl.ds(0, 1)], xs_hbm.at[pl.ds(0, 1)], sem).wait()
        return 0

    lax.fori_loop(0, tm, drain, 0)


def _dispatch_call(dest, h1, n_rows, tm=256):
    t, d = h1.shape
    xs0 = jnp.zeros((n_rows, d), F32)
    return pl.pallas_call(
        functools.partial(_dispatch_kernel, tm=tm),
        out_shape=jax.ShapeDtypeStruct((n_rows, d), F32), grid=(t // tm,),
        in_specs=[pl.BlockSpec((TOP_K, tm), lambda i: (0, i), memory_space=pltpu.SMEM),
                  pl.BlockSpec((tm, d), lambda i: (i, 0)), pl.BlockSpec(memory_space=pl.ANY)],
        out_specs=pl.BlockSpec(memory_space=pl.ANY),
        scratch_shapes=[pltpu.SemaphoreType.DMA(())],
        input_output_aliases={2: 0},
        compiler_params=pltpu.CompilerParams(dimension_semantics=("arbitrary",), has_side_effects=True),
        name="moe_dispatch",
    )(dest, h1, xs0)


def _moe_kernel(be_ref, nu_ref, fr_ref, x_ref, wgu_ref, bgu_ref, wd_ref, bd_ref, y_ref, wgu_s, wd_s):
    i = pl.program_id(0)

    @pl.when(fr_ref[i] == 1)
    def _():
        wgu_s[...] = wgu_ref[...].astype(BF16)
        wd_s[...] = wd_ref[...].astype(BF16)

    @pl.when(i < nu_ref[0])
    def _():
        gu = _dot(x_ref[...].astype(BF16), wgu_s[...]) + bgu_ref[...]
        glu = jnp.minimum(gu[:, :D_FF], SWIGLU_LIMIT)
        lin = jnp.clip(gu[:, D_FF:], -SWIGLU_LIMIT, SWIGLU_LIMIT)
        act = glu * jax.nn.sigmoid(SWIGLU_ALPHA * glu) * (lin + 1.0)
        y_ref[...] = _dot(act.astype(BF16), wd_s[...]) + bd_ref[...]

    @pl.when(i >= nu_ref[0])
    def _():
        y_ref[...] = jnp.zeros_like(y_ref)


def _moe_call(blk_e, n_used, first, xs, wgu, bgu, wd, bd):
    n_rows, d = xs.shape
    n_blk = n_rows // EXPERT_BLOCK
    grid_spec = pltpu.PrefetchScalarGridSpec(
        num_scalar_prefetch=3, grid=(n_blk,),
        in_specs=[pl.BlockSpec((EXPERT_BLOCK, d), lambda i, be, nu, fr: (i, 0)),
                  pl.BlockSpec((None, d, 2 * D_FF), lambda i, be, nu, fr: (be[i], 0, 0)),
                  pl.BlockSpec((None, 1, 2 * D_FF), lambda i, be, nu, fr: (be[i], 0, 0)),
                  pl.BlockSpec((None, D_FF, d), lambda i, be, nu, fr: (be[i], 0, 0)),
                  pl.BlockSpec((None, 1, d), lambda i, be, nu, fr: (be[i], 0, 0))],
        out_specs=pl.BlockSpec((EXPERT_BLOCK, d), lambda i, be, nu, fr: (i, 0)),
        scratch_shapes=[pltpu.VMEM((d, 2 * D_FF), BF16), pltpu.VMEM((D_FF, d), BF16)])
    return pl.pallas_call(
        _moe_kernel, out_shape=jax.ShapeDtypeStruct((n_rows, d), F32), grid_spec=grid_spec,
        compiler_params=_params(1), name="moe_experts",
    )(blk_e, n_used, first, xs, wgu, bgu, wd, bd)


def _combine_kernel(dest_ref, dnext_ref, ys_hbm, gate_ref, h1_ref, g_ref, b_ref, o_ref, buf_ref, sems, *, alpha, tm):
    i = pl.program_id(0)
    slot = lax.rem(i, 2)

    def row_copy(d_ref, s, tt, k):
        return pltpu.make_async_copy(ys_hbm.at[pl.ds(d_ref[k, tt], 1)], buf_ref.at[s, k, pl.ds(tt, 1)], sems.at[s])

    def gather(d_ref, s):
        def issue(t8, _):
            for r in range(SUBLANES):
                for k in range(TOP_K):
                    row_copy(d_ref, s, t8 * SUBLANES + r, k).start(priority=k % 2)
            return 0
        lax.fori_loop(0, tm // SUBLANES, issue, 0)

    @pl.when(i == 0)
    def _():
        gather(dest_ref, 0)

    @pl.when(i + 1 < pl.num_programs(0))
    def _():
        gather(dnext_ref, 1 - slot)

    def drain(tt, _):
        for k in range(TOP_K):
            row_copy(dest_ref, slot, 0, k).wait()
        return 0

    lax.fori_loop(0, tm, drain, 0)

    gt = gate_ref[...]
    ff = gt[:, 0:1] * buf_ref[slot, 0]
    for k in range(1, TOP_K):
        ff = ff + gt[:, k:k + 1] * buf_ref[slot, k]
    o_ref[...] = _layer_norm(alpha * h1_ref[...] + ff, g_ref[...], b_ref[...])


def _combine_call(dest, ys, gates_t, h1, g, b, alpha, tm=256):
    t, d = h1.shape
    n = t // tm
    return pl.pallas_call(
        functools.partial(_combine_kernel, alpha=alpha, tm=tm),
        out_shape=jax.ShapeDtypeStruct((t, d), F32), grid=(n,),
        in_specs=[pl.BlockSpec((TOP_K, tm), lambda i: (0, i), memory_space=pltpu.SMEM),
                  pl.BlockSpec((TOP_K, tm), lambda i: (0, jnp.minimum(i + 1, n - 1)), memory_space=pltpu.SMEM),
                  pl.BlockSpec(memory_space=pl.ANY),
                  pl.BlockSpec((tm, TOP_K), lambda i: (i, 0)), pl.BlockSpec((tm, d), lambda i: (i, 0)),
                  pl.BlockSpec((1, d), lambda i: (0, 0)), pl.BlockSpec((1, d), lambda i: (0, 0))],
        out_specs=pl.BlockSpec((tm, d), lambda i: (i, 0)),
        scratch_shapes=[pltpu.VMEM((2, TOP_K, tm, d), F32), pltpu.SemaphoreType.DMA((2,))],
        compiler_params=_params(1), name="moe_combine_ln2",
    )(dest, dest, ys, gates_t, h1, g, b)


def _prep_in_weights(w_in):
    zero = lambda n: jnp.zeros(w_in.shape[:-1] + (n,), w_in.dtype)
    col = lambda o, n: w_in[..., o:o + n]
    qidx = []
    for hd in range(IDX_HEADS):
        qidx += [col(_O_QIDX + hd * IDX_DIM, IDX_DIM), zero(LANES - IDX_DIM)]
    wzx = col(_O_Z, W_ZX)
    wq = jnp.concatenate([col(_O_QLAT, DSA_HEADS * D_LAT)] + qidx +
                         [col(_O_QMEM, D_MEMG), col(_O_KIDX, IDX_DIM), zero(LANES - IDX_DIM)], -1)
    wm = jnp.concatenate([col(_O_CKV, D_LAT), col(_O_DT, SSD_HEADS), col(_O_WIDX, IDX_HEADS),
                          zero(LANES - SSD_HEADS - IDX_HEADS)], -1)
    return wzx.astype(BF16), wq.astype(BF16), wm.astype(BF16)


def _pad_lanes(a):
    return jnp.concatenate([a, jnp.zeros(a.shape[:-1] + (LANES - a.shape[-1],), a.dtype)], -1)


def kernel(x, mem, ln_in_g, ln_in_b, w_in, conv_w, conv_b, dt_bias, a_log, d_skip, ssd_norm_g, kv_norm_g, w_uv,
           w_mem_k, w_mem_v, w_out, ln1_g, ln1_b, router_w, router_b, w_gu, b_gu, w_down, b_down, ln2_g, ln2_b):
    batch, seq, d = x.shape
    depth = w_in.shape[0]
    t = batch * seq
    alpha = (2 * depth) ** 0.25
    n_slots = t * TOP_K
    n_blk = -(-n_slots // EXPERT_BLOCK) + N_EXPERTS
    n_rows = n_blk * EXPERT_BLOCK

    wzx, wq, wm = _prep_in_weights(w_in)
    e64 = (jnp.arange(LANES)[:, None] == jnp.arange(D_SSD)[None, :] // SSD_HEAD_DIM).astype(BF16)
    e64 = jnp.concatenate([e64] * 3, 0)
    dskip_full = jnp.repeat(d_skip, SSD_HEAD_DIM, axis=-1)
    eye = jnp.eye(DSA_HEADS, dtype=w_uv.dtype)
    wuv = (w_uv[:, :, :, None, :] * eye[None, :, None, :, None]).reshape(depth, DSA_HEADS, D_LAT, D_DSA).astype(BF16)
    mem2 = mem.reshape(batch * mem.shape[1], d)
    wmk, wmv, wo = w_mem_k.astype(BF16), w_mem_v.astype(BF16), w_out.astype(BF16)
    rwt = jnp.swapaxes(router_w, 1, 2).astype(BF16)
    rb = jnp.broadcast_to(router_b[:, :, None], (depth, N_EXPERTS, LANES))

    h = _ln_call(x.reshape(t, d), ln_in_g, ln_in_b)
    for l in range(depth):
        zx, qlat, qidx, qmem, kidx, c, dtw = _inproj_call(h, wzx[l], wq[l], wm[l], kv_norm_g[l].reshape(1, D_LAT))
        y_ssd = _ssd_call(zx, dtw, conv_w[l], conv_b[l].reshape(1, CONV_DIM), _pad_lanes(dt_bias[l].reshape(1, -1)),
                          _pad_lanes(a_log[l].reshape(1, -1)), dskip_full[l].reshape(1, D_SSD),
                          ssd_norm_g[l].reshape(1, D_SSD), e64, batch)
        y_dsa = _dsa_call(qlat, qidx, dtw, c, kidx, wuv[l], batch)
        km, vm = _memkv_call(mem2, wmk[l], wmv[l], batch)
        h1, eidx, gates, rank, counts = _mix_call(y_ssd, y_dsa, qmem, km, vm, h, wo[l], ln1_g[l].reshape(1, d),
                                                  ln1_b[l].reshape(1, d), rwt[l], rb[l], batch, alpha)
        cnt = counts[:, 0]
        padded = (cnt + EXPERT_BLOCK - 1) // EXPERT_BLOCK * EXPERT_BLOCK
        pad_end = jnp.cumsum(padded)
        e_ids = jnp.arange(N_EXPERTS, dtype=I32)
        seg_start = jnp.sum(jnp.where(eidx[:, :, None] == e_ids, pad_end - padded, 0), -1)
        dest = (seg_start + rank).astype(I32)
        blk_start = jnp.arange(n_blk, dtype=I32) * EXPERT_BLOCK
        blk_e = jnp.minimum(jnp.sum((pad_end[None, :] <= blk_start[:, None]).astype(I32), -1), N_EXPERTS - 1)
        n_used = (pad_end[-1:] // EXPERT_BLOCK).astype(I32)
        xs = _dispatch_call(dest, h1, n_rows)
        first = jnp.concatenate([jnp.ones((1,), I32), (blk_e[1:] != blk_e[:-1]).astype(I32)])
        ys = _moe_call(blk_e, n_used, first, xs, w_gu[l], b_gu[l].reshape(N_EXPERTS, 1, 2 * D_FF), w_down[l],
                       b_down[l].reshape(N_EXPERTS, 1, d))
        h = _combine_call(dest, ys, gates.T, h1, ln2_g[l].reshape(1, d), ln2_b[l].reshape(1, d), alpha)
    return h.reshape(batch, seq, d)
```

```python
import functools

import jax
import jax.numpy as jnp
from jax import lax
from jax.experimental import pallas as pl
from jax.experimental.pallas import tpu as pltpu

F32, BF16, I32 = jnp.float32, jnp.bfloat16, jnp.int32
HIGHEST = lax.Precision.HIGHEST

D_MODEL = 1024
SSD_HEADS, SSD_HEAD_DIM, SSD_GROUPS, D_STATE, CONV_W, SSD_CHUNK = 16, 64, 2, 128, 4, 128
D_SSD = SSD_HEADS * SSD_HEAD_DIM
CONV_DIM = D_SSD + 2 * SSD_GROUPS * D_STATE
DSA_HEADS, DSA_HEAD_DIM, D_LAT, IDX_HEADS, IDX_DIM, TOPK_MAX = 8, 64, 128, 4, 64, 256
D_DSA = DSA_HEADS * DSA_HEAD_DIM
MEM_HEADS, MEM_HEAD_DIM = 4, 128
D_MEMG = MEM_HEADS * MEM_HEAD_DIM
D_MIX = D_SSD + D_DSA + D_MEMG
N_EXPERTS, TOP_K, D_FF = 32, 4, 1024
SWIGLU_LIMIT, SWIGLU_ALPHA = 7.0, 1.702
EXPERT_BLOCK = 512
SCORE_GROUP_LOG2, COUNT_GROUP_LOG2, TIE_GROUP_LOG2, ATTN_GROUP_LOG2 = 2, 3, 2, 2
DSA_ROW_UNIT = 128
LN_EPS, RMS_EPS = 1e-5, 1e-6

LANES = 128
SUBLANES = 8
VMEM_LIMIT = 56 * 1024 * 1024
NEG_BIG = -1e30

_O_Z, _O_XBC, _O_DT = 0, D_SSD, D_SSD + CONV_DIM
_O_QLAT = _O_DT + SSD_HEADS
_O_CKV = _O_QLAT + DSA_HEADS * D_LAT
_O_QIDX = _O_CKV + D_LAT
_O_KIDX = _O_QIDX + IDX_HEADS * IDX_DIM
_O_WIDX = _O_KIDX + IDX_DIM
_O_QMEM = _O_WIDX + IDX_HEADS
N_IN = _O_QMEM + D_MEMG
W_ZX = D_SSD + CONV_DIM
W_Q = DSA_HEADS * D_LAT + IDX_HEADS * LANES + D_MEMG + LANES
W_MISC = 2 * LANES
DTW_WIDX = SSD_HEADS


def _params(n_axes):
    return pltpu.CompilerParams(dimension_semantics=("arbitrary",) * n_axes, vmem_limit_bytes=VMEM_LIMIT)


def _layer_norm(x, g, b):
    mu = jnp.mean(x, -1, keepdims=True)
    xc = x - mu
    var = jnp.mean(xc * xc, -1, keepdims=True)
    return xc * lax.rsqrt(var + LN_EPS) * g + b


def _dot(a, b):
    return jnp.dot(a, b, preferred_element_type=F32)


def _dot_nt(a, b):
    return lax.dot_general(a, b, (((1,), (1,)), ((), ())), preferred_element_type=F32)


def _grouped_loop(lo, hi, body, init, group_log2):
    group = 1 << group_log2
    n_grp = lax.shift_right_logical(hi - lo, group_log2)

    def grp(gi, c):
        for u in range(group):
            c = body(lo + gi * group + u, c)
        return c

    c = lax.fori_loop(0, n_grp, grp, init)
    return lax.fori_loop(lo + n_grp * group, hi, body, c)


def _ln_kernel(x_ref, g_ref, b_ref, o_ref):
    o_ref[...] = _layer_norm(x_ref[...], g_ref[...], b_ref[...])


def _ln_call(x, g, b, tm=512):
    t, d = x.shape
    return pl.pallas_call(
        _ln_kernel, out_shape=jax.ShapeDtypeStruct((t, d), F32), grid=(t // tm,),
        in_specs=[pl.BlockSpec((tm, d), lambda i: (i, 0)), pl.BlockSpec((1, d), lambda i: (0, 0)),
                  pl.BlockSpec((1, d), lambda i: (0, 0))],
        out_specs=pl.BlockSpec((tm, d), lambda i: (i, 0)), compiler_params=_params(1), name="ln_in",
    )(x, g.reshape(1, d), b.reshape(1, d))


def _inproj_kernel(h_ref, wzx_ref, wq_ref, wm_ref, kvg_ref,
                   zx_ref, qlat_ref, qidx_ref, qmem_ref, kidx_ref, c_ref, dtw_ref):
    hb = h_ref[...].astype(BF16)
    zx_ref[...] = _dot(hb, wzx_ref[...])
    q = _dot(hb, wq_ref[...])
    o1 = DSA_HEADS * D_LAT
    o2 = o1 + IDX_HEADS * LANES
    o3 = o2 + D_MEMG
    qlat_ref[...] = q[:, :o1].astype(BF16)
    qidx_ref[...] = q[:, o1:o2].astype(BF16)
    qmem_ref[...] = q[:, o2:o3].astype(BF16)
    kidx_ref[...] = q[:, o3:].astype(BF16)
    misc = _dot(hb, wm_ref[...])
    ckv = misc[:, :D_LAT]
    c = ckv * lax.rsqrt(jnp.mean(ckv * ckv, -1, keepdims=True) + RMS_EPS) * kvg_ref[...]
    c_ref[...] = c.astype(BF16)
    dtw_ref[...] = misc[:, D_LAT:]


def _inproj_call(h, wzx, wq, wm, kvg, tm=512):
    t = h.shape[0]
    row = lambda w: pl.BlockSpec((tm, w), lambda i: (i, 0))
    full = lambda a: pl.BlockSpec(a.shape, lambda i: (0,) * a.ndim)
    outs = [(W_ZX, F32), (DSA_HEADS * D_LAT, BF16), (IDX_HEADS * LANES, BF16), (D_MEMG, BF16),
            (LANES, BF16), (D_LAT, BF16), (LANES, F32)]
    return pl.pallas_call(
        _inproj_kernel,
        out_shape=[jax.ShapeDtypeStruct((t, w), dt) for w, dt in outs],
        grid=(t // tm,),
        in_specs=[row(D_MODEL), full(wzx), full(wq), full(wm), full(kvg)],
        out_specs=[row(w) for w, _ in outs],
        compiler_params=_params(1), name="in_proj",
    )(h, wzx, wq, wm, kvg)


def _ssd_kernel(zx_ref, dtw_ref, cw_ref, cb_ref, dtb_ref, alog_ref, dskip_ref, ng_ref, e64_ref,
                y_ref, st_ref, ext_ref):
    q = SSD_CHUNK
    halo = SUBLANES

    @pl.when(pl.program_id(1) == 0)
    def _():
        st_ref[...] = jnp.zeros_like(st_ref)
        ext_ref[0:halo, :] = jnp.zeros((halo, CONV_DIM), F32)

    ext_ref[halo:halo + q, :] = zx_ref[:, D_SSD:]
    conv = cb_ref[...] + cw_ref[0:1, :] * ext_ref[halo - 3:halo - 3 + q, :]
    for j in range(1, CONV_W):
        conv = conv + cw_ref[j:j + 1, :] * ext_ref[halo - 3 + j:halo - 3 + j + q, :]
    ext_ref[0:halo, :] = ext_ref[q:q + halo, :]
    xbc = conv * jax.nn.sigmoid(conv)
    xs = xbc[:, :D_SSD]

    lane = lax.broadcasted_iota(I32, (q, LANES), 1)
    head_lane = lane < SSD_HEADS
    dtr = dtw_ref[...] + dtb_ref[...]
    dt = jnp.where(head_lane, jnp.maximum(dtr, 0.0) + jnp.log(1.0 + jnp.exp(-jnp.abs(dtr))), 0.0)
    adt = jnp.where(head_lane, dt * -jnp.exp(alog_ref[...]), 0.0)
    ri = lax.broadcasted_iota(I32, (q, q), 0)
    ci = lax.broadcasted_iota(I32, (q, q), 1)
    causal = ri >= ci
    acum = jnp.dot(causal.astype(F32), adt, precision=HIGHEST, preferred_element_type=F32)
    both = jnp.concatenate([dt, acum], 0)
    p1 = both.astype(BF16)
    r1 = both - p1.astype(F32)
    p2 = r1.astype(BF16)
    p3 = (r1 - p2.astype(F32)).astype(BF16)
    full = _dot(jnp.concatenate([p1, p2, p3], 1), e64_ref[...])
    dt_full, acum_full = full[:q], full[q:]
    alast = acum_full[q - 1:q, :]
    eacum = jnp.exp(acum_full)
    dend = jnp.exp(alast - acum_full)
    dlast = jnp.exp(alast)
    xdt = xs * dt_full
    acum_t = acum.T
    lo_half = lane < SSD_HEAD_DIM

    ys = []
    hpg = SSD_HEADS // SSD_GROUPS
    for g in range(SSD_GROUPS):
        bg = xbc[:, D_SSD + g * D_STATE:D_SSD + (g + 1) * D_STATE]
        cg = xbc[:, D_SSD + (SSD_GROUPS + g) * D_STATE:D_SSD + (SSD_GROUPS + g + 1) * D_STATE].astype(BF16)
        bgt = bg.T.astype(BF16)
        cbg = _dot(cg, bgt)
        for pp in range(hpg // 2):
            pair = g * (hpg // 2) + pp
            sl = slice(pair * LANES, (pair + 1) * LANES)
            ws = []
            for hh in (2 * pair, 2 * pair + 1):
                seg = acum[:, hh:hh + 1] - acum_t[hh:hh + 1, :]
                lmat = jnp.exp(jnp.where(causal, seg, -jnp.inf))
                ws.append((cbg * lmat).astype(BF16))
            xp = xdt[:, sl]
            x_lo = jnp.where(lo_half, xp, 0.0).astype(BF16)
            x_hi = jnp.where(lo_half, 0.0, xp).astype(BF16)
            y_diag = _dot(jnp.concatenate(ws, 1), jnp.concatenate([x_lo, x_hi], 0))
            st = st_ref[pair]
            y_off = _dot(cg, st.astype(BF16)) * eacum[:, sl]
            ys.append(y_diag + y_off + dskip_ref[:, sl] * xs[:, sl])
            st_ref[pair] = st * dlast[:, sl] + _dot(bgt, (xp * dend[:, sl]).astype(BF16))
    y = jnp.concatenate(ys, 1)

    z = zx_ref[:, :D_SSD]
    hf = y * (z * jax.nn.sigmoid(z))
    gw = D_SSD // SSD_GROUPS
    outs = []
    for g in range(SSD_GROUPS):
        part = hf[:, g * gw:(g + 1) * gw]
        outs.append(part * lax.rsqrt(jnp.mean(part * part, -1, keepdims=True) + RMS_EPS))
    y_ref[...] = (jnp.concatenate(outs, 1) * ng_ref[...]).astype(BF16)


def _ssd_call(zx, dtw, cw, cb, dtb, alog, dskip, ng, e64, batch):
    t = zx.shape[0]
    q = SSD_CHUNK
    nc = t // batch // q
    blk = lambda w: pl.BlockSpec((q, w), lambda b, c: (b * nc + c, 0))
    full = lambda a: pl.BlockSpec(a.shape, lambda b, c: (0,) * a.ndim)
    return pl.pallas_call(
        _ssd_kernel, out_shape=jax.ShapeDtypeStruct((t, D_SSD), BF16), grid=(batch, nc),
        in_specs=[blk(W_ZX), blk(LANES)] + [full(a) for a in (cw, cb, dtb, alog, dskip, ng, e64)],
        out_specs=blk(D_SSD),
        scratch_shapes=[pltpu.VMEM((SSD_HEADS // 2, D_STATE, LANES), F32),
                        pltpu.VMEM((q + 2 * SUBLANES, CONV_DIM), F32)],
        compiler_params=_params(2), name="ssd",
    )(zx, dtw, cw, cb, dtb, alog, dskip, ng, e64)


def _dsa_kernel(qlat_ref, qidx_ref, dtw_ref, c_ref, kidx_ref, wuv_ref, o_ref,
                key_ref, thr_ref, cnt_ref, wb_ref, m_ref, l_ref, acc_ref, *, tq, n_sel):
    i = pl.program_id(1)
    nkb = i + 1
    tk = tq
    ru = DSA_ROW_UNIT
    n_ru = tq // ru
    n_lt = tk // LANES
    i32_min = jnp.iinfo(jnp.int32).min

    w = dtw_ref[...]
    for h in range(IDX_HEADS):
        wb_ref[h] = jnp.broadcast_to(w[:, DTW_WIDX + h:DTW_WIDX + h + 1], (tq, LANES))

    def score_block(j, diag):
        kb = kidx_ref[pl.ds(pl.multiple_of(j * tk, tk), tk), :]
        sc = jnp.zeros((tq, tk), F32)
        for h in range(IDX_HEADS):
            lg = _dot_nt(qidx_ref[:, h * LANES:(h + 1) * LANES], kb)
            sc = sc + jnp.concatenate([wb_ref[h]] * n_lt, 1) * jnp.maximum(lg, 0.0)
        bits = pltpu.bitcast(sc, I32)
        key = bits ^ ((bits >> 31) & jnp.int32(0x7FFFFFFF))
        key = jnp.where(key == -1, 0, key)
        if diag:
            rows = lax.broadcasted_iota(I32, (tq, tk), 0)
            cols = lax.broadcasted_iota(I32, (tq, tk), 1)
            key = jnp.where(cols <= rows, key, i32_min)
        key_ref[j] = key

    def score_body(j, _):
        score_block(j, False)
        return 0

    _grouped_loop(0, i, score_body, 0, SCORE_GROUP_LOG2)
    score_block(i, True)

    def counts_ge(offset):
        accs = []
        for g in range(n_ru):
            rows = slice(g * ru, (g + 1) * ru)
            cand = thr_ref[rows, :] + offset

            def blk(j, acc, rows=rows, cand=cand):
                t = key_ref[j, rows, :]
                for v in range(n_lt):
                    acc = acc + jnp.where(t[:, v * LANES:(v + 1) * LANES] >= cand, 1.0, 0.0)
                return acc

            accs.append(_grouped_loop(0, nkb, blk, jnp.zeros((ru, LANES), F32), COUNT_GROUP_LOG2))
        return [jnp.broadcast_to(jnp.sum(a, axis=1, keepdims=True), (ru, LANES)) for a in accs]

    thr_ref[...] = jnp.full((tq, LANES), i32_min, I32)
    cnt_ref[...] = jnp.zeros((tq, LANES), F32) + (nkb * tk).astype(F32)

    def bit_body(b, _):
        step = jnp.left_shift(jnp.int32(1), 31 - b)
        cnts = counts_ge(step)
        for g in range(n_ru):
            rows = slice(g * ru, (g + 1) * ru)
            take = cnts[g] >= n_sel
            thr_ref[rows, :] = jnp.where(take, thr_ref[rows, :] + step, thr_ref[rows, :])
            cnt_ref[rows, :] = jnp.where(take, cnts[g], cnt_ref[rows, :])
        return 0

    lax.fori_loop(0, 32, bit_body, 0)

    thr = thr_ref[...]
    tied = jnp.where((cnt_ref[...] > n_sel) & (thr > i32_min), 1.0, 0.0)
    any_tied = jnp.max(tied) > 0.0

    @pl.when(jnp.logical_not(any_tied))
    def _():
        floor = jnp.concatenate([jnp.maximum(thr, i32_min + 1)] * n_lt, 1)

        def blk(j, _):
            key_ref[j] = pltpu.bitcast(jnp.where(key_ref[j] >= floor, 0.0, NEG_BIG), I32)
            return 0
        lax.fori_loop(0, nkb, blk, 0)

    @pl.when(any_tied)
    def _():
        cnts = counts_ge(jnp.int32(1))
        need = jnp.where(thr > i32_min, n_sel - jnp.concatenate(cnts, 0), 0.0)
        need = jnp.concatenate([need] * n_lt, 1)
        r = jnp.concatenate([thr] * n_lt, 1)
        ri = lax.broadcasted_iota(I32, (tk, tk), 0)
        ci = lax.broadcasted_iota(I32, (tk, tk), 1)
        upper = jnp.where(ri <= ci, 1.0, 0.0).astype(BF16)

        def tie_block(j, carry):
            t = key_ref[j]
            eq = t == r
            within = _dot(jnp.where(eq, 1.0, 0.0).astype(BF16), upper)
            pre = within + jnp.concatenate([carry] * n_lt, 1)
            tie = jnp.where(eq, jnp.where(pre <= need, 0.0, NEG_BIG), NEG_BIG)
            key_ref[j] = pltpu.bitcast(jnp.where(t > r, 0.0, tie), I32)
            return carry + jnp.broadcast_to(within[:, tk - 1:tk], (tq, LANES))

        _grouped_loop(0, nkb, tie_block, jnp.zeros((tq, LANES), F32), TIE_GROUP_LOG2)

    m_ref[...] = jnp.full(m_ref.shape, NEG_BIG, F32)
    l_ref[...] = jnp.zeros(l_ref.shape, F32)
    acc_ref[...] = jnp.zeros(acc_ref.shape, F32)
    log2e = 1.4426950408889634
    scale2 = D_LAT ** -0.5 * log2e
    t0 = i * tq

    def attn_blk(j, _):
        cb = c_ref[pl.ds(pl.multiple_of(j * tk, tk), tk), :]
        colpos = (j * tk - t0 + lax.broadcasted_iota(I32, (1, tk), 1)).astype(F32)
        for rh in range(n_ru):
            rows = slice(rh * ru, (rh + 1) * ru)
            for h in range(DSA_HEADS):
                slope2 = 2.0 ** (-8.0 * (h + 1) / DSA_HEADS) * log2e
                s = _dot_nt(qlat_ref[rows, h * D_LAT:(h + 1) * D_LAT], cb)
                x = s * scale2 + slope2 * colpos + pltpu.bitcast(key_ref[j, rows, :], F32)
                m_prev = m_ref[h, rows, :]
                m_next = jnp.maximum(m_prev, jnp.max(x, axis=1, keepdims=True))
                alpha = jnp.exp2(m_prev - m_next)
                p = jnp.exp2(x - jnp.concatenate([m_next] * n_lt, 1))
                psum = p[:, :LANES]
                for v in range(1, n_lt):
                    psum = psum + p[:, v * LANES:(v + 1) * LANES]
                l_ref[h, rows, :] = alpha * l_ref[h, rows, :] + psum
                acc_ref[h, rows, :] = alpha * acc_ref[h, rows, :] + _dot(p.astype(BF16), cb)
                m_ref[h, rows, :] = m_next
        return 0

    _grouped_loop(0, nkb, attn_blk, 0, ATTN_GROUP_LOG2)

    out = jnp.zeros((tq, D_DSA), F32)
    for h in range(DSA_HEADS):
        ctx = acc_ref[h] / jnp.sum(l_ref[h], axis=1, keepdims=True)
        out = out + _dot(ctx.astype(BF16), wuv_ref[h])
    o_ref[...] = out.astype(BF16)


def _dsa_call(qlat, qidx, dtw, c, kidx, wuv, batch, tq=256):
    t = qlat.shape[0]
    s = t // batch
    nq = s // tq
    n_sel = min(TOPK_MAX, s // 4)
    blk = lambda w: pl.BlockSpec((tq, w), lambda b, i: (b * nq + i, 0))
    per_b = lambda w: pl.BlockSpec((s, w), lambda b, i: (b, 0))
    return pl.pallas_call(
        functools.partial(_dsa_kernel, tq=tq, n_sel=n_sel),
        out_shape=jax.ShapeDtypeStruct((t, D_DSA), BF16), grid=(batch, nq),
        in_specs=[blk(DSA_HEADS * D_LAT), blk(IDX_HEADS * LANES), blk(LANES), per_b(D_LAT), per_b(LANES),
                  pl.BlockSpec(wuv.shape, lambda b, i: (0, 0, 0))],
        out_specs=blk(D_DSA),
        scratch_shapes=[pltpu.VMEM((nq, tq, tq), I32), pltpu.VMEM((tq, LANES), I32), pltpu.VMEM((tq, LANES), F32),
                        pltpu.VMEM((IDX_HEADS, tq, LANES), F32),
                        pltpu.VMEM((DSA_HEADS, tq, LANES), F32), pltpu.VMEM((DSA_HEADS, tq, LANES), F32),
                        pltpu.VMEM((DSA_HEADS, tq, D_LAT), F32)],
        compiler_params=_params(2), name="dsa",
    )(qlat, qidx, dtw, c, kidx, wuv)


def _memkv_kernel(mem_ref, wk_ref, wv_ref, k_ref, v_ref):
    mb = mem_ref[...].astype(BF16)
    k_ref[...] = _dot(mb, wk_ref[...]).astype(BF16)
    v_ref[...] = _dot(mb, wv_ref[...]).astype(BF16)


def _memkv_call(mem2, wk, wv, batch):
    m = mem2.shape[0] // batch
    blk = lambda w: pl.BlockSpec((m, w), lambda b: (b, 0))
    full = lambda a: pl.BlockSpec(a.shape, lambda b: (0, 0))
    return pl.pallas_call(
        _memkv_kernel, out_shape=[jax.ShapeDtypeStruct((mem2.shape[0], D_MEMG), BF16)] * 2, grid=(batch,),
        in_specs=[blk(D_MODEL), full(wk), full(wv)], out_specs=[blk(D_MEMG)] * 2,
        compiler_params=_params(1), name="mem_kv",
    )(mem2, wk, wv)


def _mix_kernel(yssd_ref, ydsa_ref, qmem_ref, km_ref, vm_ref, h_ref, wo_ref, g_ref, b_ref, rwt_ref, rb_ref,
                h1_ref, eidx_ref, gate_ref, rank_ref, cnt_ref, carry_ref, *, alpha, tm):
    @pl.when(pl.program_id(0) == 0)
    def _():
        carry_ref[...] = jnp.zeros_like(carry_ref)

    scale = MEM_HEAD_DIM ** -0.5
    ymem = []
    for hd in range(MEM_HEADS):
        sl = slice(hd * MEM_HEAD_DIM, (hd + 1) * MEM_HEAD_DIM)
        lg = _dot_nt(qmem_ref[:, sl], km_ref[:, sl]) * scale
        p = jnp.exp(lg - jnp.max(lg, axis=1, keepdims=True))
        pv = _dot(p.astype(BF16), vm_ref[:, sl])
        ymem.append((pv / jnp.sum(p, axis=1, keepdims=True)).astype(BF16))
    mix = _dot(yssd_ref[...], wo_ref[0:D_SSD, :]) + _dot(ydsa_ref[...], wo_ref[D_SSD:D_SSD + D_DSA, :])
    mix = mix + _dot(jnp.concatenate(ymem, 1), wo_ref[D_SSD + D_DSA:, :])
    h1 = _layer_norm(alpha * h_ref[...] + mix, g_ref[...], b_ref[...])
    h1_ref[...] = h1

    lt = _dot_nt(rwt_ref[...], h1.astype(BF16)) + rb_ref[:, 0:1]
    eio = lax.broadcasted_iota(I32, (N_EXPERTS, tm), 0)
    vals, idxs, hots = [], [], []
    cur = lt
    for _ in range(TOP_K):
        mx = jnp.max(cur, axis=0, keepdims=True)
        ix = jnp.min(jnp.where(cur == mx, eio, N_EXPERTS), axis=0, keepdims=True)
        hot = eio == ix
        vals.append(mx)
        idxs.append(ix)
        hots.append(hot)
        cur = jnp.where(hot, -jnp.inf, cur)
    ex = [jnp.exp(v - vals[0]) for v in vals]
    den = ex[0] + ex[1] + ex[2] + ex[3]
    multi = jnp.zeros((N_EXPERTS, tm), F32)
    for hot in hots:
        multi = multi + jnp.where(hot, 1.0, 0.0)
    ri = lax.broadcasted_iota(I32, (tm, tm), 0)
    ci = lax.broadcasted_iota(I32, (tm, tm), 1)
    before = jnp.where(ri < ci, 1.0, 0.0).astype(BF16)
    carry = carry_ref[...]
    pos = _dot(multi.astype(BF16), before) + carry[:, 0:1]
    for k in range(TOP_K):
        eidx_ref[k:k + 1, :] = idxs[k]
        gate_ref[k:k + 1, :] = ex[k] / den
        rank_ref[k:k + 1, :] = jnp.sum(jnp.where(hots[k], pos, 0.0), axis=0, keepdims=True).astype(I32)
    carry = carry + jnp.sum(multi, axis=1, keepdims=True)
    carry_ref[...] = carry
    cnt_ref[...] = carry.astype(I32)


def _mix_call(yssd, ydsa, qmem, km, vm, h, wo, g, b, rwt, rb, batch, alpha, tm=1024):
    t = h.shape[0]
    spb = t // batch // tm
    m = km.shape[0] // batch
    row = lambda w: pl.BlockSpec((tm, w), lambda i: (i, 0))
    full = lambda a: pl.BlockSpec(a.shape, lambda i: (0,) * a.ndim)
    memb = pl.BlockSpec((m, D_MEMG), lambda i: (i // spb, 0))
    tok = pl.BlockSpec((TOP_K, tm), lambda i: (0, i))
    return pl.pallas_call(
        functools.partial(_mix_kernel, alpha=alpha, tm=tm),
        out_shape=[jax.ShapeDtypeStruct((t, D_MODEL), F32), jax.ShapeDtypeStruct((TOP_K, t), I32),
                   jax.ShapeDtypeStruct((TOP_K, t), F32), jax.ShapeDtypeStruct((TOP_K, t), I32),
                   jax.ShapeDtypeStruct((N_EXPERTS, LANES), I32)],
        grid=(t // tm,),
        in_specs=[row(D_SSD), row(D_DSA), row(D_MEMG), memb, memb, row(D_MODEL), full(wo), full(g), full(b),
                  full(rwt), full(rb)],
        out_specs=[row(D_MODEL), tok, tok, tok, pl.BlockSpec((N_EXPERTS, LANES), lambda i: (0, 0))],
        scratch_shapes=[pltpu.VMEM((N_EXPERTS, LANES), F32)],
        compiler_params=_params(1), name="mix_ln1_router",
    )(yssd, ydsa, qmem, km, vm, h, wo, g, b, rwt, rb)


def _dispatch_kernel(dest_ref, h_ref, xs_in, xs_hbm, sem, *, tm):
    del xs_in

    def issue(t8, _):
        for r in range(SUBLANES):
            tt = t8 * SUBLANES + r
            for k in range(TOP_K):
                pltpu.make_async_copy(h_ref.at[pl.ds(tt, 1)], xs_hbm.at[pl.ds(dest_ref[k, tt], 1)],
                                      sem).start(priority=k % 2)
        return 0

    lax.fori_loop(0, tm // SUBLANES, issue, 0)

    def drain(tt, _):
        for k in range(TOP_K):
            pltpu.make_async_copy(h_ref.at[pl.ds(0, 1)], xs_hbm.at[pl.ds(0, 1)], sem).wait()
        return 0

    lax.fori_loop(0, tm, drain, 0)


def _dispatch_call(dest, h1, n_rows, tm=256):
    t, d = h1.shape
    xs0 = jnp.zeros((n_rows, d), F32)
    return pl.pallas_call(
        functools.partial(_dispatch_kernel, tm=tm),
        out_shape=jax.ShapeDtypeStruct((n_rows, d), F32), grid=(t // tm,),
        in_specs=[pl.BlockSpec((TOP_K, tm), lambda i: (0, i), memory_space=pltpu.SMEM),
                  pl.BlockSpec((tm, d), lambda i: (i, 0)), pl.BlockSpec(memory_space=pl.ANY)],
        out_specs=pl.BlockSpec(memory_space=pl.ANY),
        scratch_shapes=[pltpu.SemaphoreType.DMA(())],
        input_output_aliases={2: 0},
        compiler_params=pltpu.CompilerParams(dimension_semantics=("arbitrary",), has_side_effects=True),
        name="moe_dispatch",
    )(dest, h1, xs0)


def _moe_kernel(be_ref, nu_ref, fr_ref, x_ref, wgu_ref, bgu_ref, wd_ref, bd_ref, y_ref, wgu_s, wd_s):
    i = pl.program_id(0)

    @pl.when(fr_ref[i] == 1)
    def _():
        wgu_s[...] = wgu_ref[...].astype(BF16)
        wd_s[...] = wd_ref[...].astype(BF16)

    @pl.when(i < nu_ref[0])
    def _():
        gu = _dot(x_ref[...].astype(BF16), wgu_s[...]) + bgu_ref[...]
        glu = jnp.minimum(gu[:, :D_FF], SWIGLU_LIMIT)
        lin = jnp.clip(gu[:, D_FF:], -SWIGLU_LIMIT, SWIGLU_LIMIT)
        act = glu * jax.nn.sigmoid(SWIGLU_ALPHA * glu) * (lin + 1.0)
        y_ref[...] = _dot(act.astype(BF16), wd_s[...]) + bd_ref[...]

    @pl.when(i >= nu_ref[0])
    def _():
        y_ref[...] = jnp.zeros_like(y_ref)


def _moe_call(blk_e, n_used, first, xs, wgu, bgu, wd, bd):
    n_rows, d = xs.shape
    n_blk = n_rows // EXPERT_BLOCK
    grid_spec = pltpu.PrefetchScalarGridSpec(
        num_scalar_prefetch=3, grid=(n_blk,),
        in_specs=[pl.BlockSpec((EXPERT_BLOCK, d), lambda i, be, nu, fr: (i, 0)),
                  pl.BlockSpec((None, d, 2 * D_FF), lambda i, be, nu, fr: (be[i], 0, 0)),
                  pl.BlockSpec((None, 1, 2 * D_FF), lambda i, be, nu, fr: (be[i], 0, 0)),
                  pl.BlockSpec((None, D_FF, d), lambda i, be, nu, fr: (be[i], 0, 0)),
                  pl.BlockSpec((None, 1, d), lambda i, be, nu, fr: (be[i], 0, 0))],
        out_specs=pl.BlockSpec((EXPERT_BLOCK, d), lambda i, be, nu, fr: (i, 0)),
        scratch_shapes=[pltpu.VMEM((d, 2 * D_FF), BF16), pltpu.VMEM((D_FF, d), BF16)])
    return pl.pallas_call(
        _moe_kernel, out_shape=jax.ShapeDtypeStruct((n_rows, d), F32), grid_spec=grid_spec,
        compiler_params=_params(1), name="moe_experts",
    )(blk_e, n_used, first, xs, wgu, bgu, wd, bd)


def _combine_kernel(dest_ref, ys_hbm, gate_ref, h1_ref, g_ref, b_ref, o_ref, buf_ref, sem, *, alpha, tm):
    def issue(tt, _):
        for k in range(TOP_K):
            pltpu.make_async_copy(ys_hbm.at[pl.ds(dest_ref[k, tt], 1)], buf_ref.at[k, pl.ds(tt, 1)], sem).start(priority=k % 2)
        return 0

    lax.fori_loop(0, tm, issue, 0)

    def drain(tt, _):
        for k in range(TOP_K):
            pltpu.make_async_copy(ys_hbm.at[pl.ds(0, 1)], buf_ref.at[0, pl.ds(0, 1)], sem).wait()
        return 0

    lax.fori_loop(0, tm, drain, 0)

    gt = gate_ref[...]
    ff = gt[:, 0:1] * buf_ref[0]
    for k in range(1, TOP_K):
        ff = ff + gt[:, k:k + 1] * buf_ref[k]
    o_ref[...] = _layer_norm(alpha * h1_ref[...] + ff, g_ref[...], b_ref[...])


def _combine_call(dest, ys, gates_t, h1, g, b, alpha, tm=256):
    t, d = h1.shape
    return pl.pallas_call(
        functools.partial(_combine_kernel, alpha=alpha, tm=tm),
        out_shape=jax.ShapeDtypeStruct((t, d), F32), grid=(t // tm,),
        in_specs=[pl.BlockSpec((TOP_K, tm), lambda i: (0, i), memory_space=pltpu.SMEM),
                  pl.BlockSpec(memory_space=pl.ANY),
                  pl.BlockSpec((tm, TOP_K), lambda i: (i, 0)), pl.BlockSpec((tm, d), lambda i: (i, 0)),
                  pl.BlockSpec((1, d), lambda i: (0, 0)), pl.BlockSpec((1, d), lambda i: (0, 0))],
        out_specs=pl.BlockSpec((tm, d), lambda i: (i, 0)),
        scratch_shapes=[pltpu.VMEM((TOP_K, tm, d), F32), pltpu.SemaphoreType.DMA(())],
        compiler_params=_params(1), name="moe_combine_ln2",
    )(dest, ys, gates_t, h1, g, b)


def _prep_in_weights(w_in):
    zero = lambda n: jnp.zeros(w_in.shape[:-1] + (n,), w_in.dtype)
    col = lambda o, n: w_in[..., o:o + n]
    qidx = []
    for hd in range(IDX_HEADS):
        qidx += [col(_O_QIDX + hd * IDX_DIM, IDX_DIM), zero(LANES - IDX_DIM)]
    wzx = col(_O_Z, W_ZX)
    wq = jnp.concatenate([col(_O_QLAT, DSA_HEADS * D_LAT)] + qidx +
                         [col(_O_QMEM, D_MEMG), col(_O_KIDX, IDX_DIM), zero(LANES - IDX_DIM)], -1)
    wm = jnp.concatenate([col(_O_CKV, D_LAT), col(_O_DT, SSD_HEADS), col(_O_WIDX, IDX_HEADS),
                          zero(LANES - SSD_HEADS - IDX_HEADS)], -1)
    return wzx.astype(BF16), wq.astype(BF16), wm.astype(BF16)


def _pad_lanes(a):
    return jnp.concatenate([a, jnp.zeros(a.shape[:-1] + (LANES - a.shape[-1],), a.dtype)], -1)


def kernel(x, mem, ln_in_g, ln_in_b, w_in, conv_w, conv_b, dt_bias, a_log, d_skip, ssd_norm_g, kv_norm_g, w_uv,
           w_mem_k, w_mem_v, w_out, ln1_g, ln1_b, router_w, router_b, w_gu, b_gu, w_down, b_down, ln2_g, ln2_b):
    batch, seq, d = x.shape
    depth = w_in.shape[0]
    t = batch * seq
    alpha = (2 * depth) ** 0.25
    n_slots = t * TOP_K
    n_blk = -(-n_slots // EXPERT_BLOCK) + N_EXPERTS
    n_rows = n_blk * EXPERT_BLOCK

    wzx, wq, wm = _prep_in_weights(w_in)
    e64 = (jnp.arange(LANES)[:, None] == jnp.arange(D_SSD)[None, :] // SSD_HEAD_DIM).astype(BF16)
    e64 = jnp.concatenate([e64] * 3, 0)
    dskip_full = jnp.repeat(d_skip, SSD_HEAD_DIM, axis=-1)
    eye = jnp.eye(DSA_HEADS, dtype=w_uv.dtype)
    wuv = (w_uv[:, :, :, None, :] * eye[None, :, None, :, None]).reshape(depth, DSA_HEADS, D_LAT, D_DSA).astype(BF16)
    mem2 = mem.reshape(batch * mem.shape[1], d)
    wmk, wmv, wo = w_mem_k.astype(BF16), w_mem_v.astype(BF16), w_out.astype(BF16)
    rwt = jnp.swapaxes(router_w, 1, 2).astype(BF16)
    rb = jnp.broadcast_to(router_b[:, :, None], (depth, N_EXPERTS, LANES))

    h = _ln_call(x.reshape(t, d), ln_in_g, ln_in_b)
    for l in range(depth):
        zx, qlat, qidx, qmem, kidx, c, dtw = _inproj_call(h, wzx[l], wq[l], wm[l], kv_norm_g[l].reshape(1, D_LAT))
        y_ssd = _ssd_call(zx, dtw, conv_w[l], conv_b[l].reshape(1, CONV_DIM), _pad_lanes(dt_bias[l].reshape(1, -1)),
                          _pad_lanes(a_log[l].reshape(1, -1)), dskip_full[l].reshape(1, D_SSD),
                          ssd_norm_g[l].reshape(1, D_SSD), e64, batch)
        y_dsa = _dsa_call(qlat, qidx, dtw, c, kidx, wuv[l], batch)
        km, vm = _memkv_call(mem2, wmk[l], wmv[l], batch)
        h1, eidx, gates, rank, counts = _mix_call(y_ssd, y_dsa, qmem, km, vm, h, wo[l], ln1_g[l].reshape(1, d),
                                                  ln1_b[l].reshape(1, d), rwt[l], rb[l], batch, alpha)
        cnt = counts[:, 0]
        padded = (cnt + EXPERT_BLOCK - 1) // EXPERT_BLOCK * EXPERT_BLOCK
        pad_end = jnp.cumsum(padded)
        e_ids = jnp.arange(N_EXPERTS, dtype=I32)
        seg_start = jnp.sum(jnp.where(eidx[:, :, None] == e_ids, pad_end - padded, 0), -1)
        dest = (seg_start + rank).astype(I32)
        blk_start = jnp.arange(n_blk, dtype=I32) * EXPERT_BLOCK
        blk_e = jnp.minimum(jnp.sum((pad_end[None, :] <= blk_start[:, None]).astype(I32), -1), N_EXPERTS - 1)
        n_used = (pad_end[-1:] // EXPERT_BLOCK).astype(I32)
        xs = _dispatch_call(dest, h1, n_rows)
        first = jnp.concatenate([jnp.ones((1,), I32), (blk_e[1:] != blk_e[:-1]).astype(I32)])
        ys = _moe_call(blk_e, n_used, first, xs, w_gu[l], b_gu[l].reshape(N_EXPERTS, 1, 2 * D_FF), w_down[l],
                       b_down[l].reshape(N_EXPERTS, 1, d))
        h = _combine_call(dest, ys, gates.T, h1, ln2_g[l].reshape(1, d), ln2_b[l].reshape(1, d), alpha)
    return h.reshape(batch, seq, d)
```
